```python
import math
import jax, jax.numpy as jnp
from jax import lax
import numpy as np

D_MODEL = 2048
BATCH = 16
SEQ = 2048
DEPTH = 2

CHUNK = 64
Q_BLOCK = 128
N_A = DEPTH // 2
N_B = DEPTH - N_A
RWKV_HEAD = 64
RWKV_HEADS = D_MODEL // RWKV_HEAD
DECAY_LORA = max(32, int(round(1.8 * D_MODEL ** 0.5 / 32)) * 32)
AAA_LORA = max(32, int(round(1.8 * D_MODEL ** 0.5 / 32)) * 32)
GN_EPS = 64e-5
DIFF_HEADS = D_MODEL // 256
DIFF_HEAD_DIM = D_MODEL // DIFF_HEADS // 2
SUBLN_EPS = 1e-5
DEEPNORM_ALPHA = (2.0 * DEPTH) ** 0.25
DEEPNORM_BETA = (8.0 * DEPTH) ** -0.25
LN_EPS = 1e-5

kernel_name = "yoco_rwkv7_diffattn_deepnorm"


def _layernorm(x, g, b):
    xf = x.astype(jnp.float32)
    mu = jnp.mean(xf, -1, keepdims=True)
    xc = xf - mu
    var = jnp.mean(xc * xc, -1, keepdims=True)
    return (xc * lax.rsqrt(var + LN_EPS) * g + b).astype(x.dtype)


def _wkv7_scan(r, w, k, v, kk, a):
    B, T, H, N = r.shape

    def step(S, inp):
        r_t, w_t, k_t, v_t, kk_t, a_t = inp
        sa = jnp.einsum('bhvk,bhk->bhv', S, kk_t)
        S = (S * w_t[:, :, None, :]
             - sa[..., None] * (kk_t * a_t)[:, :, None, :]
             + v_t[..., None] * k_t[:, :, None, :])
        y = jnp.einsum('bhvk,bhk->bhv', S, r_t)
        return S, y

    xs = tuple(jnp.moveaxis(t.astype(jnp.float32), 1, 0) for t in (r, w, k, v, kk, a))
    S0 = jnp.zeros((B, H, N, N), jnp.float32)
    _, ys = lax.scan(step, S0, xs)
    return jnp.moveaxis(ys, 0, 1)


def _rwkv7_mixer(x, mu_proj, mu_lora, w_in, w0, w1, w2, a0, a1, a2,
                 k_k, k_a, r_k, gn_g, gn_b, w_out):
    B, T, C = x.shape
    H, N = RWKV_HEADS, RWKV_HEAD
    xx = jnp.pad(x, ((0, 0), (1, 0), (0, 0)))[:, :-1] - x
    xs = x[None] + xx[None] * mu_proj[:, None, None, :]
    proj = jnp.einsum('sbtd,sde->sbte', xs, w_in)
    r, k, v, g = proj[0], proj[1], proj[2], proj[3]
    xw = x + xx * mu_lora[0]
    xa = x + xx * mu_lora[1]
    w_log = -jax.nn.softplus(-(w0 + jnp.tanh(xw @ w1) @ w2)) - 0.5
    decay = jnp.exp(-jnp.exp(w_log.astype(jnp.float32)))
    a = jax.nn.sigmoid(a0 + (xa @ a1) @ a2)
    kk = (k * k_k).reshape(B, T, H, N).astype(jnp.float32)
    kk = kk / jnp.maximum(jnp.sqrt(jnp.sum(kk * kk, -1, keepdims=True)), 1e-12)
    k = k * (1.0 + (a - 1.0) * k_a)
    rh = r.reshape(B, T, H, N)
    kh = k.reshape(B, T, H, N)
    vh = v.reshape(B, T, H, N)
    y = _wkv7_scan(rh, decay.reshape(B, T, H, N), kh, vh, kk, a.reshape(B, T, H, N))
    mu = jnp.mean(y, -1, keepdims=True)
    yc = y - mu
    var = jnp.mean(yc * yc, -1, keepdims=True)
    y = (yc * lax.rsqrt(var + GN_EPS)).reshape(B, T, C) * gn_g + gn_b
    bonus = jnp.sum(rh.astype(jnp.float32) * kh.astype(jnp.float32) * r_k, -1, keepdims=True) * vh.astype(jnp.float32)
    y = (y + bonus.reshape(B, T, C)).astype(x.dtype)
    return (y * jax.nn.silu(g)) @ w_out


def _diff_attention(q, k, v, lam):
    B, H2, T, dh = q.shape
    H = H2 // 2
    slopes = jnp.repeat(2.0 ** (-(8.0 / H) * jnp.arange(1, H + 1, dtype=jnp.float32)), 2)
    scale = dh ** -0.5
    outs = []
    for i in range(T // Q_BLOCK):
        q0, q1 = i * Q_BLOCK, (i + 1) * Q_BLOCK
        qb = q[:, :, q0:q1]
        kb = k[:, :, :q1]
        vb = v[:, :, :q1]
        s = jnp.einsum('bmqd,bmkd->bmqk', qb, kb).astype(jnp.float32) * scale
        tq = jnp.arange(q0, q1)
        tk = jnp.arange(q1)
        dist = jnp.abs(tq[:, None] - tk[None, :]).astype(jnp.float32)
        allowed = (tk[None, :] // CHUNK) <= (tq[:, None] // CHUNK)
        s = jnp.where(allowed, s - slopes[:, None, None] * dist, -jnp.inf)
        p = jax.nn.softmax(s, axis=-1).reshape(B, H, 2, Q_BLOCK, q1)
        attn = p[:, :, 0] - lam * p[:, :, 1]
        outs.append(jnp.einsum('bhqk,bhkd->bhqd', attn.astype(vb.dtype), vb))
    return jnp.concatenate(outs, axis=2)


def _diff_mixer(x, k_sh, v_sh, w_qg, lam_p, subln_g, w_out, layer):
    B, T, C = x.shape
    H, dh = DIFF_HEADS, DIFF_HEAD_DIM
    qg = x @ w_qg
    q = qg[..., :C].reshape(B, T, 2 * H, dh).transpose(0, 2, 1, 3)
    g = qg[..., C:]
    lam_init = 0.8 - 0.6 * math.exp(-0.3 * layer)
    lam_f = lam_p.astype(jnp.float32)
    lam = (jnp.exp(jnp.sum(lam_f[0] * lam_f[1])) - jnp.exp(jnp.sum(lam_f[2] * lam_f[3])) + lam_init)
    o = _diff_attention(q, k_sh, v_sh, lam).astype(jnp.float32)
    o = o * lax.rsqrt(jnp.mean(o * o, -1, keepdims=True) + SUBLN_EPS) * subln_g * (1.0 - lam_init)
    o = o.transpose(0, 2, 1, 3).reshape(B, T, C).astype(x.dtype)
    return (o * jax.nn.silu(g)) @ w_out


def setup_inputs(seed: int = 0) -> dict:
    key = jax.random.key(seed)
    ks = jax.random.split(key, 32)
    C, H, N, dh = D_MODEL, RWKV_HEADS, RWKV_HEAD, DIFF_HEAD_DIM
    f32 = jnp.float32
    nrm = lambda k, shape, s: jax.random.normal(k, shape, f32) * s
    return {
        "x": jax.random.normal(ks[0], (BATCH, SEQ, C), f32),
        "a_mu_proj": jax.random.uniform(ks[1], (N_A, 4, C), f32),
        "a_mu_lora": jax.random.uniform(ks[2], (N_A, 2, C), f32),
        "a_w_in": nrm(ks[3], (N_A, 4, C, C), C ** -0.5),
        "a_w0": jax.random.uniform(ks[4], (N_A, C), f32, -6.0, 0.0),
        "a_w1": nrm(ks[5], (N_A, C, DECAY_LORA), C ** -0.5),
        "a_w2": nrm(ks[6], (N_A, DECAY_LORA, C), 0.1 * DECAY_LORA ** -0.5),
        "a_a0": nrm(ks[7], (N_A, C), 0.5),
        "a_a1": nrm(ks[8], (N_A, C, AAA_LORA), C ** -0.5),
        "a_a2": nrm(ks[9], (N_A, AAA_LORA, C), 0.3 * AAA_LORA ** -0.5),
        "a_k_k": 0.85 + nrm(ks[10], (N_A, C), 0.05),
        "a_k_a": 1.0 + nrm(ks[11], (N_A, C), 0.05),
        "a_r_k": -0.04 + nrm(ks[12], (N_A, H, N), 0.1),
        "a_gn_g": 1.0 + nrm(ks[13], (N_A, C), 0.05),
        "a_gn_b": nrm(ks[14], (N_A, C), 0.01),
        "a_w_out": nrm(ks[15], (N_A, C, C), C ** -0.5 * DEEPNORM_BETA),
        "w_k_shared": nrm(ks[16], (C, C), C ** -0.5),
        "w_v_shared": nrm(ks[17], (C, C), C ** -0.5),
        "b_w_qg": nrm(ks[18], (N_B, C, 2 * C), C ** -0.5),
        "b_lambda": nrm(ks[19], (N_B, 4, dh), 0.1),
        "b_subln_g": 1.0 + nrm(ks[20], (N_B, 2 * dh), 0.05),
        "b_w_out": nrm(ks[21], (N_B, C, C), C ** -0.5 * DEEPNORM_BETA),
        "ln_g": 1.0 + nrm(ks[22], (DEPTH, C), 0.05),
        "ln_b": nrm(ks[23], (DEPTH, C), 0.01),
    }


def reference(x, a_mu_proj, a_mu_lora, a_w_in, a_w0, a_w1, a_w2, a_a0, a_a1, a_a2,
              a_k_k, a_k_a, a_r_k, a_gn_g, a_gn_b, a_w_out, w_k_shared, w_v_shared,
              b_w_qg, b_lambda, b_subln_g, b_w_out, ln_g, ln_b):
    B, T, C = x.shape
    k_sh = None
    v_sh = None
    for layer in range(DEPTH):
        if layer < N_A:
            i = layer
            y = _rwkv7_mixer(x, a_mu_proj[i], a_mu_lora[i], a_w_in[i], a_w0[i], a_w1[i], a_w2[i],
                             a_a0[i], a_a1[i], a_a2[i], a_k_k[i], a_k_a[i], a_r_k[i],
                             a_gn_g[i], a_gn_b[i], a_w_out[i])
        else:
            j = layer - N_A
            y = _diff_mixer(x, k_sh, v_sh, b_w_qg[j], b_lambda[j], b_subln_g[j], b_w_out[j], layer)
        x = _layernorm(DEEPNORM_ALPHA * x + y, ln_g[layer], ln_b[layer])
        if layer == N_A - 1:
            k_sh = (x @ w_k_shared).reshape(B, T, 2 * DIFF_HEADS, DIFF_HEAD_DIM).transpose(0, 2, 1, 3)
            v_sh = (x @ w_v_shared).reshape(B, T, DIFF_HEADS, 2 * DIFF_HEAD_DIM).transpose(0, 2, 1, 3)
    return x
```

```python
import functools
import math

import jax
import jax.numpy as jnp
from jax import lax
from jax.experimental import pallas as pl
from jax.experimental.pallas import tpu as pltpu

F32 = jnp.float32
BF16 = jnp.bfloat16

DEPTH = 2
RWKV_HEAD = 64
MASK_CHUNK = 64
DIFF_HEAD_DIM = 128
GN_EPS = 64e-5
SUBLN_EPS = 1e-5
LN_EPS = 1e-5
DEEPNORM_ALPHA = (2.0 * DEPTH) ** 0.25

SLAB = 256
PAIR = 128
LORA_PAD = 128
WKV_CHUNK = 64
WKV_GROUP = 4
VMEM_LIMIT = 56 * 1024 * 1024


def _mm(a, b):
    return jnp.dot(a, b, preferred_element_type=F32)


def _mm_nt(a, b):
    return lax.dot_general(a, b, (((1,), (1,)), ((), ())), preferred_element_type=F32)


def _mm_tn(a, b):
    return lax.dot_general(a, b, (((0,), (0,)), ((), ())), preferred_element_type=F32)


def _split2(x):
    hi = x.astype(BF16)
    lo = (x - hi.astype(F32)).astype(BF16)
    return hi, lo


def _split3(x):
    hi = x.astype(BF16)
    r1 = x - hi.astype(F32)
    mid = r1.astype(BF16)
    lo = (r1 - mid.astype(F32)).astype(BF16)
    return hi, mid, lo


def _rwkv_proj_kernel(x_ref, xp_ref, mu_ref, win_ref, w0_ref, w1_ref, w2_ref, a0_ref, a1_ref, a2_ref,
                      proj_ref, lw_ref, a_ref, xs_ref, *, seq_tiles):
    i = pl.program_id(0)
    s = pl.program_id(1)
    n = pl.program_id(2)

    @pl.when((s == 0) & (n == 0))
    def _():
        x = x_ref[...]
        prev_last = xp_ref[7:8, :]
        prev_last = jnp.where((i % seq_tiles) == 0, 0.0, prev_last)
        row = lax.broadcasted_iota(jnp.int32, x.shape, 0)
        xsh = jnp.where(row == 0, prev_last, pltpu.roll(x, 1, axis=0))
        xx = xsh - x
        for st in range(4):
            xs_ref[st] = (x + xx * mu_ref[st:st + 1, :]).astype(BF16)
        xw = (x + xx * mu_ref[4:5, :]).astype(BF16)
        xa = (x + xx * mu_ref[5:6, :]).astype(BF16)
        hw = jnp.tanh(_mm(xw, w1_ref[...]))
        z = w0_ref[...] + _mm(hw.astype(BF16), w2_ref[...])
        sp = jnp.maximum(-z, 0.0) + jnp.log1p(jnp.exp(-jnp.abs(z)))
        lw = -jnp.exp(-sp - 0.5)
        ha = _mm(xa, a1_ref[...])
        za = a0_ref[...] + _mm(ha.astype(BF16), a2_ref[...])
        a = 1.0 / (1.0 + jnp.exp(-za))
        for q in range(lw_ref.shape[0]):
            lw_ref[q] = lw[:, q * SLAB:(q + 1) * SLAB]
            a_ref[q] = a[:, q * SLAB:(q + 1) * SLAB].astype(BF16)

    acc = _mm(xs_ref[s], win_ref[0])
    for q in range(proj_ref.shape[1]):
        proj_ref[0, q] = acc[:, q * SLAB:(q + 1) * SLAB].astype(BF16)


def _rwkv_proj(x2d, mu, w_in, w0, w1, w2, a0, a1, a2, *, seq_len):
    bt, c = x2d.shape
    tm = min(512, seq_len)
    tn = 512
    nq = c // SLAB
    grid = (bt // tm, 4, c // tn)
    kern = functools.partial(_rwkv_proj_kernel, seq_tiles=seq_len // tm)
    full = lambda shape: pl.BlockSpec(shape, lambda i, s, n: (0,) * len(shape))
    return pl.pallas_call(
        kern,
        grid=grid,
        in_specs=[
            pl.BlockSpec((tm, c), lambda i, s, n: (i, 0)),
            pl.BlockSpec((8, c), lambda i, s, n: (jnp.maximum(i * (tm // 8) - 1, 0), 0)),
            full((6, c)),
            pl.BlockSpec((1, c, tn), lambda i, s, n: (s, 0, n)),
            full((1, c)), full((c, LORA_PAD)), full((LORA_PAD, c)),
            full((1, c)), full((c, LORA_PAD)), full((LORA_PAD, c)),
        ],
        out_specs=[
            pl.BlockSpec((1, tn // SLAB, tm, SLAB), lambda i, s, n: (s, n, i, 0)),
            pl.BlockSpec((nq, tm, SLAB), lambda i, s, n: (0, i, 0)),
            pl.BlockSpec((nq, tm, SLAB), lambda i, s, n: (0, i, 0)),
        ],
        out_shape=[
            jax.ShapeDtypeStruct((4, nq, bt, SLAB), BF16),
            jax.ShapeDtypeStruct((nq, bt, SLAB), F32),
            jax.ShapeDtypeStruct((nq, bt, SLAB), BF16),
        ],
        scratch_shapes=[pltpu.VMEM((4, tm, c), BF16)],
        compiler_params=pltpu.CompilerParams(
            dimension_semantics=("arbitrary", "arbitrary", "arbitrary"),
            vmem_limit_bytes=VMEM_LIMIT),
    )(x2d, x2d, mu, w_in, w0, w1, w2, a0, a1, a2)


def _wkv_masks():
    ti = lax.broadcasted_iota(jnp.int32, (PAIR, PAIR), 0)
    si = lax.broadcasted_iota(jnp.int32, (PAIR, PAIR), 1)
    same = (ti // WKV_CHUNK) == (si // WKV_CHUNK)
    tl = ti % WKV_CHUNK
    sl = si % WKV_CHUNK
    levels = []
    b = 1
    while b < WKV_CHUNK:
        levels.append(same & ((tl // b) == (sl // b) + 1) & (((tl // b) % 2) == 1))
        b *= 2
    return dict(same=same, eye=ti == si,
                strict=same & (sl < tl), incl=same & (sl <= tl),
                anti_strict=(~same) & (sl < tl), anti_incl=(~same) & (sl <= tl),
                levels=levels)


def _unit_lower_inverse(nmat, masks):
    eye = jnp.where(masks["eye"], 1.0, 0.0).astype(F32)
    t = eye - jnp.where(masks["levels"][0], nmat, 0.0)
    for lvl in masks["levels"][1:]:
        nl = jnp.where(lvl, nmat, 0.0).astype(BF16)
        tb = t.astype(BF16)
        t = t - _mm(_mm(tb, nl).astype(BF16), tb)
    return t


def _wkv_kernel(r_ref, k_ref, v_ref, g_ref, lw_ref, a_ref, kk_ref, ka_ref, rk_ref, gng_ref, gnb_ref,
                o_ref, h_ref):
    seq = o_ref.shape[1]
    ch = WKV_CHUNK
    gl = WKV_GROUP * ch
    masks = _wkv_masks()
    onesbd = jnp.where(masks["same"], 1.0, 0.0).astype(BF16)
    lane = lax.broadcasted_iota(jnp.int32, (ch, PAIR), 1)
    m0 = lane < RWKV_HEAD
    gt = lax.broadcasted_iota(jnp.int32, (gl, gl), 0)
    gs = lax.broadcasted_iota(jnp.int32, (gl, gl), 1)
    tri = jnp.where(((gt // ch) == (gs // ch)) & (gs <= gt), 1.0, 0.0).astype(BF16)

    def head_sum(x):
        hi, lo = _split2(x)
        return _mm(hi, onesbd) + _mm(lo, onesbd)

    h_ref[...] = jnp.zeros(h_ref.shape, F32)

    def group(gi, carry):
        row0 = pl.multiple_of(gi * gl, gl)
        rows = pl.ds(row0, gl)
        for p in range(SLAB // PAIR):
            lanes = slice(p * PAIR, (p + 1) * PAIR)
            r = r_ref[0, 0, rows, lanes].astype(F32)
            k = k_ref[0, 0, rows, lanes].astype(F32)
            v = v_ref[0, 0, rows, lanes].astype(F32)
            g = g_ref[0, 0, rows, lanes].astype(F32)
            lw = lw_ref[0, rows, lanes]
            a = a_ref[0, rows, lanes].astype(F32)
            kk_p = kk_ref[0, :, lanes]
            ka_p = ka_ref[0, :, lanes]
            rk_p = rk_ref[0, :, lanes]

            l1, l2, l3 = _split3(lw)
            c = _mm(tri, l1) + _mm(tri, l2) + _mm(tri, l3)
            pw = jnp.exp(c)
            pinv = jnp.exp(-c)
            pex = jnp.exp(c - lw)
            kkr = k * kk_p
            kk = kkr * lax.rsqrt(jnp.maximum(head_sum(kkr * kkr), 1e-24))
            kmod = k * (1.0 + (a - 1.0) * ka_p)
            alpha = a * kk
            bonus = head_sum(r * kmod * rk_p) * v
            rt = r * pw
            bt = kk * pex
            kt = kmod * pinv
            at = alpha * pinv

            ht = h_ref[p]
            ys = []
            for j in range(WKV_GROUP):
                cs = slice(j * ch, (j + 1) * ch)
                rt_j, bt_j, kt_j, at_j, v_j = rt[cs], bt[cs], kt[cs], at[cs], v[cs]
                pl_row = pw[j * ch + ch - 1:j * ch + ch, :]
                khat = kt_j * pl_row
                ahat = at_j * pl_row
                z = jnp.zeros_like(rt_j)
                lhs0 = jnp.concatenate([jnp.where(m0, rt_j, z), jnp.where(m0, bt_j, z)], 0).astype(BF16)
                lhs1 = jnp.concatenate([jnp.where(m0, z, rt_j), jnp.where(m0, z, bt_j)], 0).astype(BF16)
                rhs0 = jnp.concatenate([at_j, kt_j], 0).astype(BF16)
                rhs1 = jnp.concatenate([kt_j, at_j], 0).astype(BF16)
                a0 = _mm_nt(lhs0, rhs0)
                a1 = _mm_nt(lhs1, rhs1)
                rblk = jnp.concatenate([a0[:ch], a1[:ch]], 0)
                bblk = jnp.concatenate([a0[ch:], a1[ch:]], 0)
                nmat = jnp.where(masks["strict"], bblk, 0.0)
                abk_anti = jnp.where(masks["anti_strict"], bblk, 0.0).astype(BF16)
                ara_bd = jnp.where(masks["incl"], rblk, 0.0).astype(BF16)
                ark_anti = jnp.where(masks["anti_incl"], rblk, 0.0).astype(BF16)
                tinv = _unit_lower_inverse(nmat, masks).astype(BF16)

                v_m0 = jnp.where(m0, v_j, z)
                v_m1 = jnp.where(m0, z, v_j)
                v_sw = jnp.concatenate([v_m1, v_m0], 0).astype(BF16)
                b_st = jnp.concatenate([jnp.where(m0, bt_j, z), jnp.where(m0, z, bt_j)], 0).astype(BF16)
                w2 = _mm(abk_anti, v_sw).astype(BF16)
                tb = _mm(tinv, jnp.concatenate([b_st, w2], 1))
                bp_st = tb[:, :PAIR]
                vp_st = tb[:, PAIR:]
                zero_st = jnp.zeros((PAIR, PAIR), BF16)
                rhs_y = jnp.concatenate([
                    jnp.concatenate([v_sw, zero_st], 1),
                    jnp.concatenate([(-vp_st).astype(BF16), (-bp_st).astype(BF16)], 1)], 0)
                yr = _mm(jnp.concatenate([ark_anti, ara_bd], 1), rhs_y)
                y0 = yr[:ch, :PAIR] + yr[ch:, :PAIR]
                rp = rt_j + yr[:ch, PAIR:] + yr[ch:, PAIR:]
                bp = bp_st[:ch] + bp_st[ch:]
                vp = vp_st[:ch] + vp_st[ch:]
                zc = jnp.zeros((ch, PAIR), BF16)
                lhs_t = jnp.concatenate([v_j, -vp, -ahat], 0).astype(BF16)
                rhs_t = jnp.concatenate([
                    jnp.concatenate([khat.astype(BF16), zc], 1),
                    jnp.concatenate([ahat.astype(BF16), zc], 1),
                    jnp.concatenate([zc, bp.astype(BF16)], 1)], 0)
                gm = _mm_tn(lhs_t, rhs_t)
                g_t = jnp.where(masks["same"], gm[:, :PAIR], 0.0)
                m2 = jnp.where(masks["same"], gm[:, PAIR:], 0.0).astype(BF16)

                htb = ht.astype(BF16)
                ys.append(_mm_nt(rp.astype(BF16), htb) + y0)
                ht = ht * pl_row + _mm_nt(htb, m2) + g_t
            h_ref[p] = ht

            y = jnp.concatenate(ys, 0)
            inv_n = 1.0 / RWKV_HEAD
            mean = head_sum(y) * inv_n
            yc = y - mean
            var = head_sum(yc * yc) * inv_n
            yn = yc * lax.rsqrt(var + GN_EPS) * gng_ref[0, :, lanes] + gnb_ref[0, :, lanes]
            out = (yn + bonus) * (g / (1.0 + jnp.exp(-g)))
            o_ref[0, rows, lanes] = out.astype(BF16)
        return carry

    lax.fori_loop(0, seq // gl, group, 0)


def _wkv(proj, lw, a, kk, ka, rk, gng, gnb, *, batch, seq_len):
    nq = lw.shape[0]
    bt = lw.shape[1]
    pspec = lambda s: pl.BlockSpec((1, 1, seq_len, SLAB), lambda b, q, s=s: (s, q, b, 0))
    aspec = pl.BlockSpec((1, seq_len, SLAB), lambda b, q: (q, b, 0))
    vspec = pl.BlockSpec((1, 1, SLAB), lambda b, q: (q, 0, 0))
    return pl.pallas_call(
        _wkv_kernel,
        grid=(batch, nq),
        in_specs=[pspec(0), pspec(1), pspec(2), pspec(3), aspec, aspec, vspec, vspec, vspec, vspec, vspec],
        out_specs=aspec,
        out_shape=jax.ShapeDtypeStruct((nq, bt, SLAB), BF16),
        scratch_shapes=[pltpu.VMEM((SLAB // PAIR, PAIR, PAIR), F32)],
        compiler_params=pltpu.CompilerParams(
            dimension_semantics=("arbitrary", "arbitrary"),
            vmem_limit_bytes=VMEM_LIMIT),
    )(proj, proj, proj, proj, lw, a, kk, ka, rk, gng, gnb)


def _out_ln_kernel(y_ref, x_ref, w_ref, g_ref, b_ref, *o_refs):
    acc = _mm(y_ref[0], w_ref[0])
    for q in range(1, y_ref.shape[0]):
        acc = acc + _mm(y_ref[q], w_ref[q])
    h = DEEPNORM_ALPHA * x_ref[...] + acc
    mu = jnp.mean(h, axis=-1, keepdims=True)
    hc = h - mu
    var = jnp.mean(hc * hc, axis=-1, keepdims=True)
    out = hc * lax.rsqrt(var + LN_EPS) * g_ref[...] + b_ref[...]
    o_refs[0][...] = out
    if len(o_refs) > 1:
        o_refs[1][...] = out.astype(BF16)


def _out_ln(y_slab, x2d, w_slab, ln_g, ln_b, *, with_bf16):
    nq, bt, _ = y_slab.shape
    c = x2d.shape[1]
    tm = min(256, bt)
    row = pl.BlockSpec((tm, c), lambda i: (i, 0))
    out_specs = [row]
    out_shape = [jax.ShapeDtypeStruct((bt, c), F32)]
    if with_bf16:
        out_specs.append(row)
        out_shape.append(jax.ShapeDtypeStruct((bt, c), BF16))
    return pl.pallas_call(
        _out_ln_kernel,
        grid=(bt // tm,),
        in_specs=[
            pl.BlockSpec((nq, tm, SLAB), lambda i: (0, i, 0)),
            row,
            pl.BlockSpec((nq, SLAB, c), lambda i: (0, 0, 0)),
            pl.BlockSpec((1, c), lambda i: (0, 0)),
            pl.BlockSpec((1, c), lambda i: (0, 0)),
        ],
        out_specs=out_specs,
        out_shape=out_shape,
        compiler_params=pltpu.CompilerParams(
            dimension_semantics=("arbitrary",),
            vmem_limit_bytes=VMEM_LIMIT),
    )(y_slab, x2d, w_slab, ln_g, ln_b)


def _slab_matmul_kernel(x_ref, w_ref, o_ref):
    acc = _mm(x_ref[...], w_ref[...])
    for q in range(o_ref.shape[0]):
        o_ref[q] = acc[:, q * SLAB:(q + 1) * SLAB].astype(BF16)


def _slab_matmul(x_bf, w_bf):
    bt, c = x_bf.shape
    n_out = w_bf.shape[1]
    tm = min(1024, bt)
    tn = 512
    return pl.pallas_call(
        _slab_matmul_kernel,
        grid=(bt // tm, n_out // tn),
        in_specs=[
            pl.BlockSpec((tm, c), lambda i, n: (i, 0)),
            pl.BlockSpec((c, tn), lambda i, n: (0, n)),
        ],
        out_specs=pl.BlockSpec((tn // SLAB, tm, SLAB), lambda i, n: (n, i, 0)),
        out_shape=jax.ShapeDtypeStruct((n_out // SLAB, bt, SLAB), BF16),
        compiler_params=pltpu.CompilerParams(
            dimension_semantics=("arbitrary", "arbitrary"),
            vmem_limit_bytes=VMEM_LIMIT),
    )(x_bf, w_bf)


def _attn_kernel(q_ref, k_ref, v_ref, g_ref, lam_ref, sg_ref, o_ref, m_ref, l_ref, acc_ref,
                 *, tq, lam_init):
    h = pl.program_id(1)
    qi = pl.program_id(2)
    dh = DIFF_HEAD_DIM
    scale = dh ** -0.5
    slope = jnp.exp2(-(h + 1).astype(F32))
    lam_p = lam_ref[...]
    lam = (jnp.exp(jnp.sum(lam_p[0:1] * lam_p[1:2])) - jnp.exp(jnp.sum(lam_p[2:3] * lam_p[3:4]))
           + lam_init)

    ii = lax.broadcasted_iota(jnp.int32, (tq, tq), 0)
    jj = lax.broadcasted_iota(jnp.int32, (tq, tq), 1)
    dmat = (ii - jj).astype(F32)
    allowed = (jj // MASK_CHUNK) <= (ii // MASK_CHUNK)

    m_ref[...] = jnp.full(m_ref.shape, -jnp.inf, F32)
    l_ref[...] = jnp.zeros(l_ref.shape, F32)
    acc_ref[...] = jnp.zeros(acc_ref.shape, F32)

    def step(j, masked):
        k0 = pl.multiple_of(j * tq, tq)
        off = ((qi - j) * tq).astype(F32)
        bias = -slope * jnp.abs(dmat + off)
        vb = v_ref[0, pl.ds(k0, tq), :]
        for u in range(2):
            lanes = slice(u * dh, (u + 1) * dh)
            s = _mm_nt(q_ref[0, :, lanes], k_ref[0, pl.ds(k0, tq), lanes]) * scale + bias
            if masked:
                s = jnp.where(allowed, s, -jnp.inf)
            m_prev = m_ref[u]
            m_new = jnp.maximum(m_prev, jnp.max(s, axis=1, keepdims=True))
            corr = jnp.exp(m_prev - m_new)
            p = jnp.exp(s - m_new)
            l_ref[u] = corr * l_ref[u] + jnp.sum(p, axis=1, keepdims=True)
            acc_ref[u] = corr * acc_ref[u] + _mm(p.astype(BF16), vb)
            m_ref[u] = m_new

    def body(j, carry):
        step(j, False)
        return carry

    lax.fori_loop(0, qi, body, 0)
    step(qi, True)

    o = acc_ref[0] / l_ref[0] - lam * (acc_ref[1] / l_ref[1])
    o = o * lax.rsqrt(jnp.mean(o * o, axis=-1, keepdims=True) + SUBLN_EPS) * sg_ref[...] * (1.0 - lam_init)
    g = g_ref[0].astype(F32)
    o_ref[0] = (o * (g / (1.0 + jnp.exp(-g)))).astype(BF16)


def _diff_attention(qgkv, lam_p, subln_g, *, batch, seq_len, layer):
    nq = qgkv.shape[0] // 4
    bt = qgkv.shape[1]
    tq = min(256, seq_len)
    nqb = seq_len // tq
    lam_init = 0.8 - 0.6 * math.exp(-0.3 * layer)
    kern = functools.partial(_attn_kernel, tq=tq, lam_init=lam_init)
    return pl.pallas_call(
        kern,
        grid=(batch, nq, nqb),
        in_specs=[
            pl.BlockSpec((1, tq, SLAB), lambda b, h, i: (h, b * nqb + i, 0)),
            pl.BlockSpec((1, seq_len, SLAB), lambda b, h, i: (2 * nq + h, b, 0)),
            pl.BlockSpec((1, seq_len, SLAB), lambda b, h, i: (3 * nq + h, b, 0)),
            pl.BlockSpec((1, tq, SLAB), lambda b, h, i: (nq + h, b * nqb + i, 0)),
            pl.BlockSpec(lam_p.shape, lambda b, h, i: (0, 0)),
            pl.BlockSpec((1, SLAB), lambda b, h, i: (0, 0)),
        ],
        out_specs=pl.BlockSpec((1, tq, SLAB), lambda b, h, i: (h, b * nqb + i, 0)),
        out_shape=jax.ShapeDtypeStruct((nq, bt, SLAB), BF16),
        scratch_shapes=[pltpu.VMEM((2, tq, 1), F32), pltpu.VMEM((2, tq, 1), F32),
                        pltpu.VMEM((2, tq, SLAB), F32)],
        compiler_params=pltpu.CompilerParams(
            dimension_semantics=("arbitrary", "arbitrary", "arbitrary"),
            vmem_limit_bytes=VMEM_LIMIT),
    )(qgkv, qgkv, qgkv, qgkv, lam_p, subln_g)


def _pad_cols(w):
    return jnp.pad(w, ((0, 0), (0, LORA_PAD - w.shape[1])))


def _pad_rows(w):
    return jnp.pad(w, ((0, LORA_PAD - w.shape[0]), (0, 0)))


def kernel(x, a_mu_proj, a_mu_lora, a_w_in, a_w0, a_w1, a_w2, a_a0, a_a1, a_a2, a_k_k, a_k_a, a_r_k,
           a_gn_g, a_gn_b, a_w_out, w_k_shared, w_v_shared, b_w_qg, b_lambda, b_subln_g, b_w_out,
           ln_g, ln_b):
    batch, seq_len, c = x.shape
    assert a_w_in.shape[0] == 1 and b_w_qg.shape[0] == 1 and ln_g.shape[0] == DEPTH
    assert c % SLAB == 0 and seq_len % (WKV_CHUNK * WKV_GROUP) == 0
    bt = batch * seq_len
    nq = c // SLAB
    x2d = x.reshape(bt, c)
    slab_vec = lambda p: p.reshape(nq, 1, SLAB)
    slab_rows = lambda w: w.astype(BF16).reshape(nq, SLAB, w.shape[1])

    mu = jnp.concatenate([a_mu_proj[0], a_mu_lora[0]], axis=0)
    proj, lw, a = _rwkv_proj(
        x2d, mu, a_w_in[0].astype(BF16),
        a_w0[0][None], _pad_cols(a_w1[0]).astype(BF16), _pad_rows(a_w2[0]).astype(BF16),
        a_a0[0][None], _pad_cols(a_a1[0]).astype(BF16), _pad_rows(a_a2[0]).astype(BF16),
        seq_len=seq_len)
    yg = _wkv(proj, lw, a, slab_vec(a_k_k[0]), slab_vec(a_k_a[0]), slab_vec(a_r_k[0]),
              slab_vec(a_gn_g[0]), slab_vec(a_gn_b[0]), batch=batch, seq_len=seq_len)
    x1, x1_bf = _out_ln(yg, x2d, slab_rows(a_w_out[0]), ln_g[0][None], ln_b[0][None], with_bf16=True)

    w_all = jnp.concatenate([b_w_qg[0], w_k_shared, w_v_shared], axis=1).astype(BF16)
    qgkv = _slab_matmul(x1_bf, w_all)
    og = _diff_attention(qgkv, b_lambda[0], b_subln_g[0][None], batch=batch, seq_len=seq_len, layer=1)
    (out,) = _out_ln(og, x1, slab_rows(b_w_out[0]), ln_g[1][None], ln_b[1][None], with_bf16=False)
    return out.reshape(batch, seq_len, c)
```

```python
import functools
import math

import jax
import jax.numpy as jnp
from jax import lax
from jax.experimental import pallas as pl
from jax.experimental.pallas import tpu as pltpu

F32 = jnp.float32
BF16 = jnp.bfloat16

DEPTH = 2
RWKV_HEAD = 64
MASK_CHUNK = 64
DIFF_HEAD_DIM = 128
GN_EPS = 64e-5
SUBLN_EPS = 1e-5
LN_EPS = 1e-5
DEEPNORM_ALPHA = (2.0 * DEPTH) ** 0.25

SLAB = 256
PAIR = 128
LORA_PAD = 128
WKV_CHUNK = 64
WKV_GROUP = 4
VMEM_LIMIT = 56 * 1024 * 1024


def _mm(a, b):
    return jnp.dot(a, b, preferred_element_type=F32)


def _mm_nt(a, b):
    return lax.dot_general(a, b, (((1,), (1,)), ((), ())), preferred_element_type=F32)


def _mm_tn(a, b):
    return lax.dot_general(a, b, (((0,), (0,)), ((), ())), preferred_element_type=F32)


def _split2(x):
    hi = x.astype(BF16)
    lo = (x - hi.astype(F32)).astype(BF16)
    return hi, lo


def _split3(x):
    hi = x.astype(BF16)
    r1 = x - hi.astype(F32)
    mid = r1.astype(BF16)
    lo = (r1 - mid.astype(F32)).astype(BF16)
    return hi, mid, lo


def _rwkv_proj_kernel(x_ref, xp_ref, mu_ref, win_ref, w0_ref, w1_ref, w2_ref, a0_ref, a1_ref, a2_ref,
                      proj_ref, lw_ref, a_ref, xs_ref, *, seq_tiles):
    i = pl.program_id(0)
    s = pl.program_id(1)
    n = pl.program_id(2)

    @pl.when((s == 0) & (n == 0))
    def _():
        x = x_ref[...]
        prev_last = xp_ref[7:8, :]
        prev_last = jnp.where((i % seq_tiles) == 0, 0.0, prev_last)
        row = lax.broadcasted_iota(jnp.int32, x.shape, 0)
        xsh = jnp.where(row == 0, prev_last, pltpu.roll(x, 1, axis=0))
        xx = xsh - x
        for st in range(4):
            xs_ref[st] = (x + xx * mu_ref[st:st + 1, :]).astype(BF16)
        xw = (x + xx * mu_ref[4:5, :]).astype(BF16)
        xa = (x + xx * mu_ref[5:6, :]).astype(BF16)
        hw = jnp.tanh(_mm(xw, w1_ref[...]))
        z = w0_ref[...] + _mm(hw.astype(BF16), w2_ref[...])
        sp = jnp.maximum(-z, 0.0) + jnp.log1p(jnp.exp(-jnp.abs(z)))
        lw = -jnp.exp(-sp - 0.5)
        ha = _mm(xa, a1_ref[...])
        za = a0_ref[...] + _mm(ha.astype(BF16), a2_ref[...])
        a = 1.0 / (1.0 + jnp.exp(-za))
        for q in range(lw_ref.shape[0]):
            lw_ref[q] = lw[:, q * SLAB:(q + 1) * SLAB]
            a_ref[q] = a[:, q * SLAB:(q + 1) * SLAB].astype(BF16)

    acc = _mm(xs_ref[s], win_ref[0])
    for q in range(proj_ref.shape[1]):
        proj_ref[0, q] = acc[:, q * SLAB:(q + 1) * SLAB].astype(BF16)


def _rwkv_proj(x2d, mu, w_in, w0, w1, w2, a0, a1, a2, *, seq_len):
    bt, c = x2d.shape
    tm = min(512, seq_len)
    tn = 512
    nq = c // SLAB
    grid = (bt // tm, 4, c // tn)
    kern = functools.partial(_rwkv_proj_kernel, seq_tiles=seq_len // tm)
    full = lambda shape: pl.BlockSpec(shape, lambda i, s, n: (0,) * len(shape))
    return pl.pallas_call(
        kern,
        grid=grid,
        in_specs=[
            pl.BlockSpec((tm, c), lambda i, s, n: (i, 0)),
            pl.BlockSpec((8, c), lambda i, s, n: (jnp.maximum(i * (tm // 8) - 1, 0), 0)),
            full((6, c)),
            pl.BlockSpec((1, c, tn), lambda i, s, n: (s, 0, n)),
            full((1, c)), full((c, LORA_PAD)), full((LORA_PAD, c)),
            full((1, c)), full((c, LORA_PAD)), full((LORA_PAD, c)),
        ],
        out_specs=[
            pl.BlockSpec((1, tn // SLAB, tm, SLAB), lambda i, s, n: (s, n, i, 0)),
            pl.BlockSpec((nq, tm, SLAB), lambda i, s, n: (0, i, 0)),
            pl.BlockSpec((nq, tm, SLAB), lambda i, s, n: (0, i, 0)),
        ],
        out_shape=[
            jax.ShapeDtypeStruct((4, nq, bt, SLAB), BF16),
            jax.ShapeDtypeStruct((nq, bt, SLAB), F32),
            jax.ShapeDtypeStruct((nq, bt, SLAB), BF16),
        ],
        scratch_shapes=[pltpu.VMEM((4, tm, c), BF16)],
        compiler_params=pltpu.CompilerParams(
            dimension_semantics=("arbitrary", "arbitrary", "arbitrary"),
            vmem_limit_bytes=VMEM_LIMIT),
    )(x2d, x2d, mu, w_in, w0, w1, w2, a0, a1, a2)


def _wkv_masks():
    ti = lax.broadcasted_iota(jnp.int32, (PAIR, PAIR), 0)
    si = lax.broadcasted_iota(jnp.int32, (PAIR, PAIR), 1)
    same = (ti // WKV_CHUNK) == (si // WKV_CHUNK)
    tl = ti % WKV_CHUNK
    sl = si % WKV_CHUNK
    levels = []
    b = 1
    while b < WKV_CHUNK:
        levels.append(same & ((tl // b) == (sl // b) + 1) & (((tl // b) % 2) == 1))
        b *= 2
    return dict(same=same, eye=ti == si,
                strict=same & (sl < tl), incl=same & (sl <= tl),
                anti_strict=(~same) & (sl < tl), anti_incl=(~same) & (sl <= tl),
                levels=levels)


def _wkv_kernel(r_ref, k_ref, v_ref, g_ref, lw_ref, a_ref, kk_ref, ka_ref, rk_ref, gng_ref, gnb_ref,
                o_ref, h_ref):
    seq = o_ref.shape[1]
    ch = WKV_CHUNK
    gl = WKV_GROUP * ch
    masks = _wkv_masks()
    onesbd = jnp.where(masks["same"], 1.0, 0.0).astype(BF16)
    lane = lax.broadcasted_iota(jnp.int32, (ch, PAIR), 1)
    m0 = lane < RWKV_HEAD
    gt = lax.broadcasted_iota(jnp.int32, (gl, gl), 0)
    gs = lax.broadcasted_iota(jnp.int32, (gl, gl), 1)
    tri = jnp.where(((gt // ch) == (gs // ch)) & (gs <= gt), 1.0, 0.0).astype(BF16)

    def head_sum(x):
        hi, lo = _split2(x)
        return _mm(hi, onesbd) + _mm(lo, onesbd)

    h_ref[...] = jnp.zeros(h_ref.shape, F32)

    npair = SLAB // PAIR
    inv_n = 1.0 / RWKV_HEAD

    def group(gi, carry):
        row0 = pl.multiple_of(gi * gl, gl)
        rows = pl.ds(row0, gl)
        pre = []
        for p in range(npair):
            lanes = slice(p * PAIR, (p + 1) * PAIR)
            r = r_ref[0, 0, rows, lanes].astype(F32)
            k = k_ref[0, 0, rows, lanes].astype(F32)
            v = v_ref[0, 0, rows, lanes].astype(F32)
            lw = lw_ref[0, rows, lanes]
            a = a_ref[0, rows, lanes].astype(F32)
            l1, l2, l3 = _split3(lw)
            c = _mm(tri, l1) + _mm(tri, l2) + _mm(tri, l3)
            pw = jnp.exp(c)
            pinv = jnp.exp(-c)
            pex = jnp.exp(c - lw)
            kkr = k * kk_ref[0, :, lanes]
            kk = kkr * lax.rsqrt(jnp.maximum(head_sum(kkr * kkr), 1e-24))
            kmod = k * (1.0 + (a - 1.0) * ka_ref[0, :, lanes])
            bonus = head_sum(r * kmod * rk_ref[0, :, lanes]) * v
            pre.append(dict(v=v, pw=pw, rt=r * pw, bt=kk * pex, kt=kmod * pinv, at=a * kk * pinv,
                            bonus=bonus))

        units = []
        for p in range(npair):
            d = pre[p]
            for j in range(WKV_GROUP):
                cs = slice(j * ch, (j + 1) * ch)
                units.append(dict(rt=d["rt"][cs], bt=d["bt"][cs], kt=d["kt"][cs], at=d["at"][cs],
                                  v=d["v"][cs], pl_row=d["pw"][j * ch + ch - 1:j * ch + ch, :]))

        z = jnp.zeros((ch, PAIR), F32)
        for u in units:
            rt_j, bt_j, kt_j, at_j = u["rt"], u["bt"], u["kt"], u["at"]
            lhs0 = jnp.concatenate([jnp.where(m0, rt_j, z), jnp.where(m0, bt_j, z)], 0).astype(BF16)
            lhs1 = jnp.concatenate([jnp.where(m0, z, rt_j), jnp.where(m0, z, bt_j)], 0).astype(BF16)
            rhs0 = jnp.concatenate([at_j, kt_j], 0).astype(BF16)
            rhs1 = jnp.concatenate([kt_j, at_j], 0).astype(BF16)
            u["a0"] = _mm_nt(lhs0, rhs0)
            u["a1"] = _mm_nt(lhs1, rhs1)
        for u in units:
            a0, a1 = u.pop("a0"), u.pop("a1")
            rblk = jnp.concatenate([a0[:ch], a1[:ch]], 0)
            bblk = jnp.concatenate([a0[ch:], a1[ch:]], 0)
            u["nmat"] = jnp.where(masks["strict"], bblk, 0.0)
            u["abk_anti"] = jnp.where(masks["anti_strict"], bblk, 0.0).astype(BF16)
            u["ara_bd"] = jnp.where(masks["incl"], rblk, 0.0).astype(BF16)
            u["ark_anti"] = jnp.where(masks["anti_incl"], rblk, 0.0).astype(BF16)
            v_j, bt_j = u["v"], u["bt"]
            u["v_sw"] = jnp.concatenate([jnp.where(m0, z, v_j), jnp.where(m0, v_j, z)], 0).astype(BF16)
            u["b_st"] = jnp.concatenate([jnp.where(m0, bt_j, z), jnp.where(m0, z, bt_j)], 0).astype(BF16)
        for u in units:
            u["w2"] = _mm(u["abk_anti"], u["v_sw"]).astype(BF16)

        eye = jnp.where(masks["eye"], 1.0, 0.0).astype(F32)
        for u in units:
            u["t"] = eye - jnp.where(masks["levels"][0], u["nmat"], 0.0)
        for lvl in masks["levels"][1:]:
            for u in units:
                u["tb"] = u["t"].astype(BF16)
                u["x"] = _mm(u["tb"], jnp.where(lvl, u["nmat"], 0.0).astype(BF16)).astype(BF16)
            for u in units:
                u["t"] = u["t"] - _mm(u.pop("x"), u.pop("tb"))

        for u in units:
            u["tbv"] = _mm(u["t"].astype(BF16), jnp.concatenate([u["b_st"], u["w2"]], 1))
        zero_st = jnp.zeros((PAIR, PAIR), BF16)
        for u in units:
            tbv = u["tbv"]
            bp_st = tbv[:, :PAIR]
            vp_st = tbv[:, PAIR:]
            rhs_y = jnp.concatenate([
                jnp.concatenate([u["v_sw"], zero_st], 1),
                jnp.concatenate([(-vp_st).astype(BF16), (-bp_st).astype(BF16)], 1)], 0)
            u["yr"] = _mm(jnp.concatenate([u["ark_anti"], u["ara_bd"]], 1), rhs_y)
            bp = bp_st[:ch] + bp_st[ch:]
            vp = vp_st[:ch] + vp_st[ch:]
            khat = u["kt"] * u["pl_row"]
            ahat = u["at"] * u["pl_row"]
            zc = jnp.zeros((ch, PAIR), BF16)
            lhs_t = jnp.concatenate([u["v"], -vp, -ahat], 0).astype(BF16)
            rhs_t = jnp.concatenate([
                jnp.concatenate([khat.astype(BF16), zc], 1),
                jnp.concatenate([ahat.astype(BF16), zc], 1),
                jnp.concatenate([zc, bp.astype(BF16)], 1)], 0)
            u["gm"] = _mm_tn(lhs_t, rhs_t)
        for u in units:
            yr, gm = u["yr"], u["gm"]
            u["y0"] = yr[:ch, :PAIR] + yr[ch:, :PAIR]
            u["rp"] = (u["rt"] + yr[:ch, PAIR:] + yr[ch:, PAIR:]).astype(BF16)
            u["g_t"] = jnp.where(masks["same"], gm[:, :PAIR], 0.0)
            u["m2"] = jnp.where(masks["same"], gm[:, PAIR:], 0.0).astype(BF16)

        hts = [h_ref[p] for p in range(npair)]
        ys = [[] for _ in range(npair)]
        for j in range(WKV_GROUP):
            for p in range(npair):
                u = units[p * WKV_GROUP + j]
                htb = hts[p].astype(BF16)
                ys[p].append(_mm_nt(u["rp"], htb) + u["y0"])
                hts[p] = hts[p] * u["pl_row"] + _mm_nt(htb, u["m2"]) + u["g_t"]
        for p in range(npair):
            h_ref[p] = hts[p]

        for p in range(npair):
            lanes = slice(p * PAIR, (p + 1) * PAIR)
            y = jnp.concatenate(ys[p], 0)
            mean = head_sum(y) * inv_n
            yc = y - mean
            var = head_sum(yc * yc) * inv_n
            yn = yc * lax.rsqrt(var + GN_EPS) * gng_ref[0, :, lanes] + gnb_ref[0, :, lanes]
            g = g_ref[0, 0, rows, lanes].astype(F32)
            out = (yn + pre[p]["bonus"]) * (g / (1.0 + jnp.exp(-g)))
            o_ref[0, rows, lanes] = out.astype(BF16)
        return carry

    lax.fori_loop(0, seq // gl, group, 0)


def _wkv(proj, lw, a, kk, ka, rk, gng, gnb, *, batch, seq_len):
    nq = lw.shape[0]
    bt = lw.shape[1]
    pspec = lambda s: pl.BlockSpec((1, 1, seq_len, SLAB), lambda b, q, s=s: (s, q, b, 0))
    aspec = pl.BlockSpec((1, seq_len, SLAB), lambda b, q: (q, b, 0))
    vspec = pl.BlockSpec((1, 1, SLAB), lambda b, q: (q, 0, 0))
    return pl.pallas_call(
        _wkv_kernel,
        grid=(batch, nq),
        in_specs=[pspec(0), pspec(1), pspec(2), pspec(3), aspec, aspec, vspec, vspec, vspec, vspec, vspec],
        out_specs=aspec,
        out_shape=jax.ShapeDtypeStruct((nq, bt, SLAB), BF16),
        scratch_shapes=[pltpu.VMEM((SLAB // PAIR, PAIR, PAIR), F32)],
        compiler_params=pltpu.CompilerParams(
            dimension_semantics=("arbitrary", "arbitrary"),
            vmem_limit_bytes=VMEM_LIMIT),
    )(proj, proj, proj, proj, lw, a, kk, ka, rk, gng, gnb)


def _out_ln_kernel(y_ref, x_ref, w_ref, g_ref, b_ref, *o_refs):
    acc = _mm(y_ref[0], w_ref[0])
    for q in range(1, y_ref.shape[0]):
        acc = acc + _mm(y_ref[q], w_ref[q])
    h = DEEPNORM_ALPHA * x_ref[...] + acc
    mu = jnp.mean(h, axis=-1, keepdims=True)
    hc = h - mu
    var = jnp.mean(hc * hc, axis=-1, keepdims=True)
    out = hc * lax.rsqrt(var + LN_EPS) * g_ref[...] + b_ref[...]
    o_refs[0][...] = out
    if len(o_refs) > 1:
        o_refs[1][...] = out.astype(BF16)


def _out_ln(y_slab, x2d, w_slab, ln_g, ln_b, *, with_bf16):
    nq, bt, _ = y_slab.shape
    c = x2d.shape[1]
    tm = min(256, bt)
    row = pl.BlockSpec((tm, c), lambda i: (i, 0))
    out_specs = [row]
    out_shape = [jax.ShapeDtypeStruct((bt, c), F32)]
    if with_bf16:
        out_specs.append(row)
        out_shape.append(jax.ShapeDtypeStruct((bt, c), BF16))
    return pl.pallas_call(
        _out_ln_kernel,
        grid=(bt // tm,),
        in_specs=[
            pl.BlockSpec((nq, tm, SLAB), lambda i: (0, i, 0)),
            row,
            pl.BlockSpec((nq, SLAB, c), lambda i: (0, 0, 0)),
            pl.BlockSpec((1, c), lambda i: (0, 0)),
            pl.BlockSpec((1, c), lambda i: (0, 0)),
        ],
        out_specs=out_specs,
        out_shape=out_shape,
        compiler_params=pltpu.CompilerParams(
            dimension_semantics=("arbitrary",),
            vmem_limit_bytes=VMEM_LIMIT),
    )(y_slab, x2d, w_slab, ln_g, ln_b)


def _slab_matmul_kernel(x_ref, w_ref, o_ref, *, q_tiles):
    acc = _mm(x_ref[...], w_ref[...])
    acc = acc * jnp.where(pl.program_id(1) < q_tiles, DIFF_HEAD_DIM ** -0.5, 1.0)
    for q in range(o_ref.shape[0]):
        o_ref[q] = acc[:, q * SLAB:(q + 1) * SLAB].astype(BF16)


def _slab_matmul(x_bf, w_bf, *, q_cols):
    bt, c = x_bf.shape
    n_out = w_bf.shape[1]
    tm = min(1024, bt)
    tn = 512
    return pl.pallas_call(
        functools.partial(_slab_matmul_kernel, q_tiles=q_cols // tn),
        grid=(bt // tm, n_out // tn),
        in_specs=[
            pl.BlockSpec((tm, c), lambda i, n: (i, 0)),
            pl.BlockSpec((c, tn), lambda i, n: (0, n)),
        ],
        out_specs=pl.BlockSpec((tn // SLAB, tm, SLAB), lambda i, n: (n, i, 0)),
        out_shape=jax.ShapeDtypeStruct((n_out // SLAB, bt, SLAB), BF16),
        compiler_params=pltpu.CompilerParams(
            dimension_semantics=("arbitrary", "arbitrary"),
            vmem_limit_bytes=VMEM_LIMIT),
    )(x_bf, w_bf)


def _attn_kernel(q_ref, k_ref, v_ref, g_ref, lam_ref, sg_ref, o_ref, m_ref, l_ref, acc_ref,
                 *, tq, lam_init):
    h = pl.program_id(1)
    qi = pl.program_id(2)
    dh = DIFF_HEAD_DIM
    slope = jnp.exp2(-(h + 1).astype(F32))
    lam_p = lam_ref[...]
    lam = (jnp.exp(jnp.sum(lam_p[0:1] * lam_p[1:2])) - jnp.exp(jnp.sum(lam_p[2:3] * lam_p[3:4]))
           + lam_init)

    ii = lax.broadcasted_iota(jnp.int32, (tq, tq), 0)
    jj = lax.broadcasted_iota(jnp.int32, (tq, tq), 1)
    dmat = (ii - jj).astype(F32)
    allowed = (jj // MASK_CHUNK) <= (ii // MASK_CHUNK)

    m_ref[...] = jnp.full(m_ref.shape, -jnp.inf, F32)
    l_ref[...] = jnp.zeros(l_ref.shape, F32)
    acc_ref[...] = jnp.zeros(acc_ref.shape, F32)

    nl = tq // 128

    def step(blocks):
        kbs, vbs, biases = [], [], []
        for j, masked in blocks:
            k0 = pl.multiple_of(j * tq, tq)
            if masked:
                biases.append(slope * ((k0 + ii).astype(F32) - jnp.abs(dmat)))
            else:
                biases.append(slope * (k0 + jj[0:1, :]).astype(F32))
            kbs.append(k_ref[0, pl.ds(k0, tq), :])
            vbs.append(v_ref[0, pl.ds(k0, tq), :])
        s = [[None] * len(blocks) for _ in range(2)]
        for u in range(2):
            lanes = slice(u * dh, (u + 1) * dh)
            for b, (j, masked) in enumerate(blocks):
                su = _mm_nt(q_ref[0, :, lanes], kbs[b][:, lanes]) + biases[b]
                if masked:
                    su = jnp.where(allowed, su, -jnp.inf)
                s[u][b] = su
        m_new, p = [], [[None] * len(blocks) for _ in range(2)]
        for u in range(2):
            pieces = [s[u][b][:, c * 128:(c + 1) * 128] for b in range(len(blocks)) for c in range(nl)]
            smax = functools.reduce(jnp.maximum, pieces)
            m_new.append(jnp.maximum(m_ref[u], jnp.max(smax, axis=1, keepdims=True)))
            mrep = jnp.concatenate([m_new[u]] * nl, axis=1)
            for b in range(len(blocks)):
                p[u][b] = jnp.exp(s[u][b] - mrep)
        pv = [sum(_mm(p[u][b].astype(BF16), vbs[b]) for b in range(len(blocks))) for u in range(2)]
        for u in range(2):
            pieces = [p[u][b][:, c * 128:(c + 1) * 128] for b in range(len(blocks)) for c in range(nl)]
            psum = jnp.sum(functools.reduce(jnp.add, pieces), axis=1, keepdims=True)
            corr = jnp.exp(m_ref[u] - m_new[u])
            l_ref[u] = corr * l_ref[u] + psum
            acc_ref[u] = jnp.concatenate([corr] * (SLAB // 128), axis=1) * acc_ref[u] + pv[u]
            m_ref[u] = m_new[u]

    def body(jp, carry):
        step([(2 * jp, False), (2 * jp + 1, False)])
        return carry

    lax.fori_loop(0, qi // 2, body, 0)

    @pl.when(qi % 2 == 1)
    def _():
        step([(qi - 1, False), (qi, True)])

    @pl.when(qi % 2 == 0)
    def _():
        step([(qi, True)])

    inv_l = [1.0 / l_ref[u][:, 0:1] for u in range(2)]
    o = acc_ref[0] * inv_l[0] - lam * (acc_ref[1] * inv_l[1])
    o = o * lax.rsqrt(jnp.mean(o * o, axis=-1, keepdims=True) + SUBLN_EPS) * sg_ref[...] * (1.0 - lam_init)
    g = g_ref[0].astype(F32)
    o_ref[0] = (o * (g / (1.0 + jnp.exp(-g)))).astype(BF16)


def _diff_attention(qgkv, lam_p, subln_g, *, batch, seq_len, layer):
    nq = qgkv.shape[0] // 4
    bt = qgkv.shape[1]
    tq = min(256, seq_len)
    nqb = seq_len // tq
    lam_init = 0.8 - 0.6 * math.exp(-0.3 * layer)
    kern = functools.partial(_attn_kernel, tq=tq, lam_init=lam_init)
    return pl.pallas_call(
        kern,
        grid=(batch, nq, nqb),
        in_specs=[
            pl.BlockSpec((1, tq, SLAB), lambda b, h, i: (h, b * nqb + i, 0)),
            pl.BlockSpec((1, seq_len, SLAB), lambda b, h, i: (2 * nq + h, b, 0)),
            pl.BlockSpec((1, seq_len, SLAB), lambda b, h, i: (3 * nq + h, b, 0)),
            pl.BlockSpec((1, tq, SLAB), lambda b, h, i: (nq + h, b * nqb + i, 0)),
            pl.BlockSpec(lam_p.shape, lambda b, h, i: (0, 0)),
            pl.BlockSpec((1, SLAB), lambda b, h, i: (0, 0)),
        ],
        out_specs=pl.BlockSpec((1, tq, SLAB), lambda b, h, i: (h, b * nqb + i, 0)),
        out_shape=jax.ShapeDtypeStruct((nq, bt, SLAB), BF16),
        scratch_shapes=[pltpu.VMEM((2, tq, 128), F32), pltpu.VMEM((2, tq, 128), F32),
                        pltpu.VMEM((2, tq, SLAB), F32)],
        compiler_params=pltpu.CompilerParams(
            dimension_semantics=("arbitrary", "arbitrary", "arbitrary"),
            vmem_limit_bytes=VMEM_LIMIT),
    )(qgkv, qgkv, qgkv, qgkv, lam_p, subln_g)


def _pad_cols(w):
    return jnp.pad(w, ((0, 0), (0, LORA_PAD - w.shape[1])))


def _pad_rows(w):
    return jnp.pad(w, ((0, LORA_PAD - w.shape[0]), (0, 0)))


def kernel(x, a_mu_proj, a_mu_lora, a_w_in, a_w0, a_w1, a_w2, a_a0, a_a1, a_a2, a_k_k, a_k_a, a_r_k,
           a_gn_g, a_gn_b, a_w_out, w_k_shared, w_v_shared, b_w_qg, b_lambda, b_subln_g, b_w_out,
           ln_g, ln_b):
    batch, seq_len, c = x.shape
    assert a_w_in.shape[0] == 1 and b_w_qg.shape[0] == 1 and ln_g.shape[0] == DEPTH
    assert c % SLAB == 0 and seq_len % (WKV_CHUNK * WKV_GROUP) == 0
    bt = batch * seq_len
    nq = c // SLAB
    x2d = x.reshape(bt, c)
    slab_vec = lambda p: p.reshape(nq, 1, SLAB)
    slab_rows = lambda w: w.astype(BF16).reshape(nq, SLAB, w.shape[1])

    mu = jnp.concatenate([a_mu_proj[0], a_mu_lora[0]], axis=0)
    proj, lw, a = _rwkv_proj(
        x2d, mu, a_w_in[0].astype(BF16),
        a_w0[0][None], _pad_cols(a_w1[0]).astype(BF16), _pad_rows(a_w2[0]).astype(BF16),
        a_a0[0][None], _pad_cols(a_a1[0]).astype(BF16), _pad_rows(a_a2[0]).astype(BF16),
        seq_len=seq_len)
    yg = _wkv(proj, lw, a, slab_vec(a_k_k[0]), slab_vec(a_k_a[0]), slab_vec(a_r_k[0]),
              slab_vec(a_gn_g[0]), slab_vec(a_gn_b[0]), batch=batch, seq_len=seq_len)
    x1, x1_bf = _out_ln(yg, x2d, slab_rows(a_w_out[0]), ln_g[0][None], ln_b[0][None], with_bf16=True)

    w_all = jnp.concatenate([b_w_qg[0], w_k_shared, w_v_shared], axis=1).astype(BF16)
    qgkv = _slab_matmul(x1_bf, w_all, q_cols=c)
    og = _diff_attention(qgkv, b_lambda[0], b_subln_g[0][None], batch=batch, seq_len=seq_len, layer=1)
    (out,) = _out_ln(og, x1, slab_rows(b_w_out[0]), ln_g[1][None], ln_b[1][None], with_bf16=False)
    return out.reshape(batch, seq_len, c)
```

```python
import functools
import math

import jax
import jax.numpy as jnp
from jax import lax
from jax.experimental import pallas as pl
from jax.experimental.pallas import tpu as pltpu

F32 = jnp.float32
BF16 = jnp.bfloat16

DEPTH = 2
RWKV_HEAD = 64
MASK_CHUNK = 64
DIFF_HEAD_DIM = 128
GN_EPS = 64e-5
SUBLN_EPS = 1e-5
LN_EPS = 1e-5
DEEPNORM_ALPHA = (2.0 * DEPTH) ** 0.25

SLAB = 256
PAIR = 128
LORA_PAD = 128
WKV_CHUNK = 64
WKV_GROUP = 4
VMEM_LIMIT = 56 * 1024 * 1024


def _mm(a, b):
    return jnp.dot(a, b, preferred_element_type=F32)


def _mm_nt(a, b):
    return lax.dot_general(a, b, (((1,), (1,)), ((), ())), preferred_element_type=F32)


def _mm_tn(a, b):
    return lax.dot_general(a, b, (((0,), (0,)), ((), ())), preferred_element_type=F32)


def _split2(x):
    hi = x.astype(BF16)
    lo = (x - hi.astype(F32)).astype(BF16)
    return hi, lo


def _rwkv_mix_kernel(x_ref, xp_ref, mu_ref, w0_ref, w1_ref, w2_ref, a0_ref, a1_ref, a2_ref,
                     xs_ref, lw_ref, a_ref, *, seq_tiles):
    i = pl.program_id(0)
    x = x_ref[...]
    prev_last = xp_ref[7:8, :]
    prev_last = jnp.where((i % seq_tiles) == 0, 0.0, prev_last)
    row = lax.broadcasted_iota(jnp.int32, x.shape, 0)
    xsh = jnp.where(row == 0, prev_last, pltpu.roll(x, 1, axis=0))
    xx = xsh - x
    for st in range(4):
        xs_ref[st] = (x + xx * mu_ref[st:st + 1, :]).astype(BF16)
    xw = (x + xx * mu_ref[4:5, :]).astype(BF16)
    xa = (x + xx * mu_ref[5:6, :]).astype(BF16)
    hw = jnp.tanh(_mm(xw, w1_ref[...]))
    z = w0_ref[...] + _mm(hw.astype(BF16), w2_ref[...])
    sp = jnp.maximum(-z, 0.0) + jnp.log1p(jnp.exp(-jnp.abs(z)))
    lw = -jnp.exp(-sp - 0.5)
    ha = _mm(xa, a1_ref[...])
    za = a0_ref[...] + _mm(ha.astype(BF16), a2_ref[...])
    a = 1.0 / (1.0 + jnp.exp(-za))
    for q in range(lw_ref.shape[0]):
        lw_ref[q] = lw[:, q * SLAB:(q + 1) * SLAB]
        a_ref[q] = a[:, q * SLAB:(q + 1) * SLAB].astype(BF16)


def _rwkv_mix(x2d, mu, w0, w1, w2, a0, a1, a2, *, seq_len):
    bt, c = x2d.shape
    tm = min(256, seq_len)
    nq = c // SLAB
    kern = functools.partial(_rwkv_mix_kernel, seq_tiles=seq_len // tm)
    full = lambda shape: pl.BlockSpec(shape, lambda i: (0,) * len(shape))
    return pl.pallas_call(
        kern,
        grid=(bt // tm,),
        in_specs=[
            pl.BlockSpec((tm, c), lambda i: (i, 0)),
            pl.BlockSpec((8, c), lambda i: (jnp.maximum(i * (tm // 8) - 1, 0), 0)),
            full((6, c)),
            full((1, c)), full((c, LORA_PAD)), full((LORA_PAD, c)),
            full((1, c)), full((c, LORA_PAD)), full((LORA_PAD, c)),
        ],
        out_specs=[
            pl.BlockSpec((4, tm, c), lambda i: (0, i, 0)),
            pl.BlockSpec((nq, tm, SLAB), lambda i: (0, i, 0)),
            pl.BlockSpec((nq, tm, SLAB), lambda i: (0, i, 0)),
        ],
        out_shape=[
            jax.ShapeDtypeStruct((4, bt, c), BF16),
            jax.ShapeDtypeStruct((nq, bt, SLAB), F32),
            jax.ShapeDtypeStruct((nq, bt, SLAB), BF16),
        ],
        compiler_params=pltpu.CompilerParams(
            dimension_semantics=("arbitrary",),
            vmem_limit_bytes=VMEM_LIMIT),
    )(x2d, x2d, mu, w0, w1, w2, a0, a1, a2)


def _stream_matmul_kernel(x_ref, w_ref, o_ref):
    acc = _mm(x_ref[0], w_ref[0])
    for q in range(o_ref.shape[1]):
        o_ref[0, q] = acc[:, q * SLAB:(q + 1) * SLAB].astype(BF16)


def _stream_matmul(xs, w_in):
    ns, bt, c = xs.shape
    tm = min(1024, bt)
    tn = 512
    return pl.pallas_call(
        _stream_matmul_kernel,
        grid=(bt // tm, ns, c // tn),
        in_specs=[
            pl.BlockSpec((1, tm, c), lambda i, s, n: (s, i, 0)),
            pl.BlockSpec((1, c, tn), lambda i, s, n: (s, 0, n)),
        ],
        out_specs=pl.BlockSpec((1, tn // SLAB, tm, SLAB), lambda i, s, n: (s, n, i, 0)),
        out_shape=jax.ShapeDtypeStruct((ns, c // SLAB, bt, SLAB), BF16),
        compiler_params=pltpu.CompilerParams(
            dimension_semantics=("arbitrary", "arbitrary", "arbitrary"),
            vmem_limit_bytes=VMEM_LIMIT),
    )(xs, w_in)


def _wkv_masks():
    ti = lax.broadcasted_iota(jnp.int32, (PAIR, PAIR), 0)
    si = lax.broadcasted_iota(jnp.int32, (PAIR, PAIR), 1)
    same = (ti // WKV_CHUNK) == (si // WKV_CHUNK)
    tl = ti % WKV_CHUNK
    sl = si % WKV_CHUNK
    levels = []
    b = 1
    while b < WKV_CHUNK:
        levels.append(same & ((tl // b) == (sl // b) + 1) & (((tl // b) % 2) == 1))
        b *= 2
    return dict(same=same, eye=ti == si,
                strict=same & (sl < tl), incl=same & (sl <= tl),
                anti_strict=(~same) & (sl < tl), anti_incl=(~same) & (sl <= tl),
                levels=levels)


def _wkv_kernel(r_ref, k_ref, v_ref, g_ref, lw_ref, a_ref, kk_ref, ka_ref, rk_ref, gng_ref, gnb_ref,
                o_ref, h_ref):
    seq = o_ref.shape[1]
    ch = WKV_CHUNK
    gl = WKV_GROUP * ch
    masks = _wkv_masks()
    onesbd = jnp.where(masks["same"], 1.0, 0.0).astype(BF16)
    lane = lax.broadcasted_iota(jnp.int32, (ch, PAIR), 1)
    m0 = lane < RWKV_HEAD
    gt = lax.broadcasted_iota(jnp.int32, (gl, gl), 0)
    gs = lax.broadcasted_iota(jnp.int32, (gl, gl), 1)
    tri = jnp.where(((gt // ch) == (gs // ch)) & (gs <= gt), 1.0, 0.0).astype(BF16)

    def head_sum(x):
        return _mm(x.astype(BF16), onesbd)

    h_ref[...] = jnp.zeros(h_ref.shape, F32)

    npair = SLAB // PAIR
    inv_n = 1.0 / RWKV_HEAD

    def group(gi, carry):
        row0 = pl.multiple_of(gi * gl, gl)
        rows = pl.ds(row0, gl)
        pre = []
        for p in range(npair):
            lanes = slice(p * PAIR, (p + 1) * PAIR)
            r = r_ref[0, 0, rows, lanes].astype(F32)
            k = k_ref[0, 0, rows, lanes].astype(F32)
            v = v_ref[0, 0, rows, lanes].astype(F32)
            lw = lw_ref[0, rows, lanes]
            a = a_ref[0, rows, lanes].astype(F32)
            l1, l2 = _split2(lw)
            c = _mm(tri, l1) + _mm(tri, l2)
            pw = jnp.exp(c)
            pinv = jnp.exp(-c)
            pex = jnp.exp(c - lw)
            kkr = k * kk_ref[0, :, lanes]
            kk = kkr * lax.rsqrt(jnp.maximum(head_sum(kkr * kkr), 1e-24))
            kmod = k * (1.0 + (a - 1.0) * ka_ref[0, :, lanes])
            bonus = head_sum(r * kmod * rk_ref[0, :, lanes]) * v
            pre.append(dict(v=v, pw=pw, rt=r * pw, bt=kk * pex, kt=kmod * pinv, at=a * kk * pinv,
                            bonus=bonus))

        units = []
        for p in range(npair):
            d = pre[p]
            for j in range(WKV_GROUP):
                cs = slice(j * ch, (j + 1) * ch)
                units.append(dict(rt=d["rt"][cs], bt=d["bt"][cs], kt=d["kt"][cs], at=d["at"][cs],
                                  v=d["v"][cs], pl_row=d["pw"][j * ch + ch - 1:j * ch + ch, :]))

        z = jnp.zeros((ch, PAIR), F32)
        for u in units:
            rt_j, bt_j, kt_j, at_j = u["rt"], u["bt"], u["kt"], u["at"]
            lhs0 = jnp.concatenate([jnp.where(m0, rt_j, z), jnp.where(m0, bt_j, z)], 0).astype(BF16)
            lhs1 = jnp.concatenate([jnp.where(m0, z, rt_j), jnp.where(m0, z, bt_j)], 0).astype(BF16)
            rhs0 = jnp.concatenate([at_j, kt_j], 0).astype(BF16)
            rhs1 = jnp.concatenate([kt_j, at_j], 0).astype(BF16)
            u["a0"] = _mm_nt(lhs0, rhs0)
            u["a1"] = _mm_nt(lhs1, rhs1)
        for u in units:
            a0, a1 = u.pop("a0"), u.pop("a1")
            rblk = jnp.concatenate([a0[:ch], a1[:ch]], 0)
            bblk = jnp.concatenate([a0[ch:], a1[ch:]], 0)
            u["nmat"] = jnp.where(masks["strict"], bblk, 0.0)
            u["abk_anti"] = jnp.where(masks["anti_strict"], bblk, 0.0).astype(BF16)
            u["ara_bd"] = jnp.where(masks["incl"], rblk, 0.0).astype(BF16)
            u["ark_anti"] = jnp.where(masks["anti_incl"], rblk, 0.0).astype(BF16)
            v_j, bt_j = u["v"], u["bt"]
            u["v_sw"] = jnp.concatenate([jnp.where(m0, z, v_j), jnp.where(m0, v_j, z)], 0).astype(BF16)
            u["b_st"] = jnp.concatenate([jnp.where(m0, bt_j, z), jnp.where(m0, z, bt_j)], 0).astype(BF16)
        for u in units:
            u["w2"] = _mm(u["abk_anti"], u["v_sw"]).astype(BF16)

        eye = jnp.where(masks["eye"], 1.0, 0.0).astype(F32)
        for u in units:
            u["t"] = eye - jnp.where(masks["levels"][0], u["nmat"], 0.0)
        for lvl in masks["levels"][1:]:
            for u in units:
                u["tb"] = u["t"].astype(BF16)
                u["x"] = _mm(u["tb"], jnp.where(lvl, u["nmat"], 0.0).astype(BF16)).astype(BF16)
            for u in units:
                u["t"] = u["t"] - _mm(u.pop("x"), u.pop("tb"))

        for u in units:
            u["tbv"] = _mm(u["t"].astype(BF16), jnp.concatenate([u["b_st"], u["w2"]], 1))
        zero_st = jnp.zeros((PAIR, PAIR), BF16)
        for u in units:
            tbv = u["tbv"]
            bp_st = tbv[:, :PAIR]
            vp_st = tbv[:, PAIR:]
            rhs_y = jnp.concatenate([
                jnp.concatenate([u["v_sw"], zero_st], 1),
                jnp.concatenate([(-vp_st).astype(BF16), (-bp_st).astype(BF16)], 1)], 0)
            u["yr"] = _mm(jnp.concatenate([u["ark_anti"], u["ara_bd"]], 1), rhs_y)
            bp = bp_st[:ch] + bp_st[ch:]
            vp = vp_st[:ch] + vp_st[ch:]
            khat = u["kt"] * u["pl_row"]
            ahat = u["at"] * u["pl_row"]
            zc = jnp.zeros((ch, PAIR), BF16)
            lhs_t = jnp.concatenate([u["v"], -vp, -ahat], 0).astype(BF16)
            rhs_t = jnp.concatenate([
                jnp.concatenate([khat.astype(BF16), zc], 1),
                jnp.concatenate([ahat.astype(BF16), zc], 1),
                jnp.concatenate([zc, bp.astype(BF16)], 1)], 0)
            u["gm"] = _mm_tn(lhs_t, rhs_t)
        for u in units:
            yr, gm = u["yr"], u["gm"]
            u["y0"] = yr[:ch, :PAIR] + yr[ch:, :PAIR]
            u["rp"] = (u["rt"] + yr[:ch, PAIR:] + yr[ch:, PAIR:]).astype(BF16)
            u["g_t"] = jnp.where(masks["same"], gm[:, :PAIR], 0.0)
            u["m2"] = jnp.where(masks["same"], gm[:, PAIR:], 0.0).astype(BF16)

        hts = [h_ref[p] for p in range(npair)]
        ys = [[] for _ in range(npair)]
        for j in range(WKV_GROUP):
            for p in range(npair):
                u = units[p * WKV_GROUP + j]
                htb = hts[p].astype(BF16)
                ys[p].append(_mm_nt(u["rp"], htb) + u["y0"])
                hts[p] = hts[p] * u["pl_row"] + _mm_nt(htb, u["m2"]) + u["g_t"]
        for p in range(npair):
            h_ref[p] = hts[p]

        for p in range(npair):
            lanes = slice(p * PAIR, (p + 1) * PAIR)
            y = jnp.concatenate(ys[p], 0)
            mean = head_sum(y) * inv_n
            yc = y - mean
            var = head_sum(yc * yc) * inv_n
            yn = yc * lax.rsqrt(var + GN_EPS) * gng_ref[0, :, lanes] + gnb_ref[0, :, lanes]
            g = g_ref[0, 0, rows, lanes].astype(F32)
            out = (yn + pre[p]["bonus"]) * (g / (1.0 + jnp.exp(-g)))
            o_ref[0, rows, lanes] = out.astype(BF16)
        return carry

    lax.fori_loop(0, seq // gl, group, 0)


def _wkv(proj, lw, a, kk, ka, rk, gng, gnb, *, batch, seq_len):
    nq = lw.shape[0]
    bt = lw.shape[1]
    pspec = lambda s: pl.BlockSpec((1, 1, seq_len, SLAB), lambda b, q, s=s: (s, q, b, 0))
    aspec = pl.BlockSpec((1, seq_len, SLAB), lambda b, q: (q, b, 0))
    vspec = pl.BlockSpec((1, 1, SLAB), lambda b, q: (q, 0, 0))
    return pl.pallas_call(
        _wkv_kernel,
        grid=(batch, nq),
        in_specs=[pspec(0), pspec(1), pspec(2), pspec(3), aspec, aspec, vspec, vspec, vspec, vspec, vspec],
        out_specs=aspec,
        out_shape=jax.ShapeDtypeStruct((nq, bt, SLAB), BF16),
        scratch_shapes=[pltpu.VMEM((SLAB // PAIR, PAIR, PAIR), F32)],
        compiler_params=pltpu.CompilerParams(
            dimension_semantics=("arbitrary", "arbitrary"),
            vmem_limit_bytes=VMEM_LIMIT),
    )(proj, proj, proj, proj, lw, a, kk, ka, rk, gng, gnb)


def _out_ln_kernel(y_ref, x_ref, w_ref, g_ref, b_ref, *o_refs):
    acc = _mm(y_ref[0], w_ref[0])
    for q in range(1, y_ref.shape[0]):
        acc = acc + _mm(y_ref[q], w_ref[q])
    h = DEEPNORM_ALPHA * x_ref[...] + acc
    mu = jnp.mean(h, axis=-1, keepdims=True)
    hc = h - mu
    var = jnp.mean(hc * hc, axis=-1, keepdims=True)
    out = hc * lax.rsqrt(var + LN_EPS) * g_ref[...] + b_ref[...]
    o_refs[0][...] = out
    if len(o_refs) > 1:
        o_refs[1][...] = out.astype(BF16)


def _out_ln(y_slab, x2d, w_slab, ln_g, ln_b, *, with_bf16):
    nq, bt, _ = y_slab.shape
    c = x2d.shape[1]
    tm = min(256, bt)
    row = pl.BlockSpec((tm, c), lambda i: (i, 0))
    out_specs = [row]
    out_shape = [jax.ShapeDtypeStruct((bt, c), F32)]
    if with_bf16:
        out_specs.append(row)
        out_shape.append(jax.ShapeDtypeStruct((bt, c), BF16))
    return pl.pallas_call(
        _out_ln_kernel,
        grid=(bt // tm,),
        in_specs=[
            pl.BlockSpec((nq, tm, SLAB), lambda i: (0, i, 0)),
            row,
            pl.BlockSpec((nq, SLAB, c), lambda i: (0, 0, 0)),
            pl.BlockSpec((1, c), lambda i: (0, 0)),
            pl.BlockSpec((1, c), lambda i: (0, 0)),
        ],
        out_specs=out_specs,
        out_shape=out_shape,
        compiler_params=pltpu.CompilerParams(
            dimension_semantics=("arbitrary",),
            vmem_limit_bytes=VMEM_LIMIT),
    )(y_slab, x2d, w_slab, ln_g, ln_b)


def _slab_matmul_kernel(x_ref, w_ref, o_ref, *, q_tiles):
    acc = _mm(x_ref[...], w_ref[...])
    acc = acc * jnp.where(pl.program_id(1) < q_tiles, DIFF_HEAD_DIM ** -0.5, 1.0)
    for q in range(o_ref.shape[0]):
        o_ref[q] = acc[:, q * SLAB:(q + 1) * SLAB].astype(BF16)


def _slab_matmul(x_bf, w_bf, *, q_cols):
    bt, c = x_bf.shape
    n_out = w_bf.shape[1]
    tm = min(1024, bt)
    tn = 512
    return pl.pallas_call(
        functools.partial(_slab_matmul_kernel, q_tiles=q_cols // tn),
        grid=(bt // tm, n_out // tn),
        in_specs=[
            pl.BlockSpec((tm, c), lambda i, n: (i, 0)),
            pl.BlockSpec((c, tn), lambda i, n: (0, n)),
        ],
        out_specs=pl.BlockSpec((tn // SLAB, tm, SLAB), lambda i, n: (n, i, 0)),
        out_shape=jax.ShapeDtypeStruct((n_out // SLAB, bt, SLAB), BF16),
        compiler_params=pltpu.CompilerParams(
            dimension_semantics=("arbitrary", "arbitrary"),
            vmem_limit_bytes=VMEM_LIMIT),
    )(x_bf, w_bf)


def _attn_kernel(q_ref, k_ref, v_ref, g_ref, lam_ref, sg_ref, o_ref, m_ref, l_ref, acc_ref,
                 *, tq, lam_init):
    h = pl.program_id(1)
    qi = pl.program_id(2)
    dh = DIFF_HEAD_DIM
    slope = jnp.exp2(-(h + 1).astype(F32))
    lam_p = lam_ref[...]
    lam = (jnp.exp(jnp.sum(lam_p[0:1] * lam_p[1:2])) - jnp.exp(jnp.sum(lam_p[2:3] * lam_p[3:4]))
           + lam_init)

    ii = lax.broadcasted_iota(jnp.int32, (tq, tq), 0)
    jj = lax.broadcasted_iota(jnp.int32, (tq, tq), 1)
    dmat = (ii - jj).astype(F32)
    allowed = (jj // MASK_CHUNK) <= (ii // MASK_CHUNK)

    m_ref[...] = jnp.full(m_ref.shape, -jnp.inf, F32)
    l_ref[...] = jnp.zeros(l_ref.shape, F32)
    acc_ref[...] = jnp.zeros(acc_ref.shape, F32)

    nl = tq // 128
    ones_k = jnp.ones((tq, 128), BF16)

    def step(blocks):
        kbs, vbs, biases = [], [], []
        for j, masked in blocks:
            k0 = pl.multiple_of(j * tq, tq)
            if masked:
                biases.append(slope * ((k0 + ii).astype(F32) - jnp.abs(dmat)))
            else:
                biases.append(slope * (k0 + jj[0:1, :]).astype(F32))
            kbs.append(k_ref[0, pl.ds(k0, tq), :])
            vbs.append(v_ref[0, pl.ds(k0, tq), :])
        s = [[None] * len(blocks) for _ in range(2)]
        for u in range(2):
            lanes = slice(u * dh, (u + 1) * dh)
            for b, (j, masked) in enumerate(blocks):
                su = _mm_nt(q_ref[0, :, lanes], kbs[b][:, lanes]) + biases[b]
                if masked:
                    su = jnp.where(allowed, su, -jnp.inf)
                s[u][b] = su
        m_new, p = [], [[None] * len(blocks) for _ in range(2)]
        for u in range(2):
            pieces = [s[u][b][:, c * 128:(c + 1) * 128] for b in range(len(blocks)) for c in range(nl)]
            smax = functools.reduce(jnp.maximum, pieces)
            m_new.append(jnp.maximum(m_ref[u], jnp.max(smax, axis=1, keepdims=True)))
            mrep = jnp.concatenate([m_new[u]] * nl, axis=1)
            for b in range(len(blocks)):
                p[u][b] = jnp.exp(s[u][b] - mrep).astype(BF16)
        pv = [sum(_mm(p[u][b], vbs[b]) for b in range(len(blocks))) for u in range(2)]
        ps = [sum(_mm(p[u][b], ones_k) for b in range(len(blocks))) for u in range(2)]
        for u in range(2):
            psum = ps[u]
            corr = jnp.exp(m_ref[u] - m_new[u])
            l_ref[u] = corr * l_ref[u] + psum
            acc_ref[u] = jnp.concatenate([corr] * (SLAB // 128), axis=1) * acc_ref[u] + pv[u]
            m_ref[u] = m_new[u]

    def body(jp, carry):
        step([(2 * jp, False), (2 * jp + 1, False)])
        return carry

    lax.fori_loop(0, qi // 2, body, 0)

    @pl.when(qi % 2 == 1)
    def _():
        step([(qi - 1, False), (qi, True)])

    @pl.when(qi % 2 == 0)
    def _():
        step([(qi, True)])

    inv_l = [1.0 / l_ref[u][:, 0:1] for u in range(2)]
    o = acc_ref[0] * inv_l[0] - lam * (acc_ref[1] * inv_l[1])
    o = o * lax.rsqrt(jnp.mean(o * o, axis=-1, keepdims=True) + SUBLN_EPS) * sg_ref[...] * (1.0 - lam_init)
    g = g_ref[0].astype(F32)
    o_ref[0] = (o * (g / (1.0 + jnp.exp(-g)))).astype(BF16)


def _diff_attention(qgkv, lam_p, subln_g, *, batch, seq_len, layer):
    nq = qgkv.shape[0] // 4
    bt = qgkv.shape[1]
    tq = min(256, seq_len)
    nqb = seq_len // tq
    lam_init = 0.8 - 0.6 * math.exp(-0.3 * layer)
    kern = functools.partial(_attn_kernel, tq=tq, lam_init=lam_init)
    return pl.pallas_call(
        kern,
        grid=(batch, nq, nqb),
        in_specs=[
            pl.BlockSpec((1, tq, SLAB), lambda b, h, i: (h, b * nqb + i, 0)),
            pl.BlockSpec((1, seq_len, SLAB), lambda b, h, i: (2 * nq + h, b, 0)),
            pl.BlockSpec((1, seq_len, SLAB), lambda b, h, i: (3 * nq + h, b, 0)),
            pl.BlockSpec((1, tq, SLAB), lambda b, h, i: (nq + h, b * nqb + i, 0)),
            pl.BlockSpec(lam_p.shape, lambda b, h, i: (0, 0)),
            pl.BlockSpec((1, SLAB), lambda b, h, i: (0, 0)),
        ],
        out_specs=pl.BlockSpec((1, tq, SLAB), lambda b, h, i: (h, b * nqb + i, 0)),
        out_shape=jax.ShapeDtypeStruct((nq, bt, SLAB), BF16),
        scratch_shapes=[pltpu.VMEM((2, tq, 128), F32), pltpu.VMEM((2, tq, 128), F32),
                        pltpu.VMEM((2, tq, SLAB), F32)],
        compiler_params=pltpu.CompilerParams(
            dimension_semantics=("arbitrary", "arbitrary", "arbitrary"),
            vmem_limit_bytes=VMEM_LIMIT),
    )(qgkv, qgkv, qgkv, qgkv, lam_p, subln_g)


def _pad_cols(w):
    return jnp.pad(w, ((0, 0), (0, LORA_PAD - w.shape[1])))


def _pad_rows(w):
    return jnp.pad(w, ((0, LORA_PAD - w.shape[0]), (0, 0)))


def kernel(x, a_mu_proj, a_mu_lora, a_w_in, a_w0, a_w1, a_w2, a_a0, a_a1, a_a2, a_k_k, a_k_a, a_r_k,
           a_gn_g, a_gn_b, a_w_out, w_k_shared, w_v_shared, b_w_qg, b_lambda, b_subln_g, b_w_out,
           ln_g, ln_b):
    batch, seq_len, c = x.shape
    assert a_w_in.shape[0] == 1 and b_w_qg.shape[0] == 1 and ln_g.shape[0] == DEPTH
    assert c % SLAB == 0 and seq_len % (WKV_CHUNK * WKV_GROUP) == 0
    bt = batch * seq_len
    nq = c // SLAB
    x2d = x.reshape(bt, c)
    slab_vec = lambda p: p.reshape(nq, 1, SLAB)
    slab_rows = lambda w: w.astype(BF16).reshape(nq, SLAB, w.shape[1])

    mu = jnp.concatenate([a_mu_proj[0], a_mu_lora[0]], axis=0)
    xs, lw, a = _rwkv_mix(
        x2d, mu,
        a_w0[0][None], _pad_cols(a_w1[0]).astype(BF16), _pad_rows(a_w2[0]).astype(BF16),
        a_a0[0][None], _pad_cols(a_a1[0]).astype(BF16), _pad_rows(a_a2[0]).astype(BF16),
        seq_len=seq_len)
    proj = _stream_matmul(xs, a_w_in[0].astype(BF16))
    yg = _wkv(proj, lw, a, slab_vec(a_k_k[0]), slab_vec(a_k_a[0]), slab_vec(a_r_k[0]),
              slab_vec(a_gn_g[0]), slab_vec(a_gn_b[0]), batch=batch, seq_len=seq_len)
    x1, x1_bf = _out_ln(yg, x2d, slab_rows(a_w_out[0]), ln_g[0][None], ln_b[0][None], with_bf16=True)

    w_all = jnp.concatenate([b_w_qg[0], w_k_shared, w_v_shared], axis=1).astype(BF16)
    qgkv = _slab_matmul(x1_bf, w_all, q_cols=c)
    og = _diff_attention(qgkv, b_lambda[0], b_subln_g[0][None], batch=batch, seq_len=seq_len, layer=1)
    (out,) = _out_ln(og, x1, slab_rows(b_w_out[0]), ln_g[1][None], ln_b[1][None], with_bf16=False)
    return out.reshape(batch, seq_len, c)
```

```python
import functools
import math

import jax
import jax.numpy as jnp
from jax import lax
from jax.experimental import pallas as pl
from jax.experimental.pallas import tpu as pltpu

F32 = jnp.float32
BF16 = jnp.bfloat16

DEPTH = 2
RWKV_HEAD = 64
MASK_CHUNK = 64
DIFF_HEAD_DIM = 128
GN_EPS = 64e-5
SUBLN_EPS = 1e-5
LN_EPS = 1e-5
DEEPNORM_ALPHA = (2.0 * DEPTH) ** 0.25

SLAB = 256
PAIR = 128
LORA_PAD = 128
WKV_CHUNK = 64
WKV_GROUP = 4
VMEM_LIMIT = 56 * 1024 * 1024


def _mm(a, b):
    return jnp.dot(a, b, preferred_element_type=F32)


def _mm_nt(a, b):
    return lax.dot_general(a, b, (((1,), (1,)), ((), ())), preferred_element_type=F32)


def _mm_tn(a, b):
    return lax.dot_general(a, b, (((0,), (0,)), ((), ())), preferred_element_type=F32)


def _split2(x):
    hi = x.astype(BF16)
    lo = (x - hi.astype(F32)).astype(BF16)
    return hi, lo


def _rwkv_mix_kernel(x_ref, xp_ref, mu_ref, w0_ref, w1_ref, w2_ref, a0_ref, a1_ref, a2_ref,
                     xs_ref, lw_ref, a_ref, *, seq_tiles):
    i = pl.program_id(0)
    x = x_ref[...]
    prev_last = xp_ref[7:8, :]
    prev_last = jnp.where((i % seq_tiles) == 0, 0.0, prev_last)
    row = lax.broadcasted_iota(jnp.int32, x.shape, 0)
    xsh = jnp.where(row == 0, prev_last, pltpu.roll(x, 1, axis=0))
    xx = xsh - x
    for st in range(4):
        xs_ref[st] = (x + xx * mu_ref[st:st + 1, :]).astype(BF16)
    xw = (x + xx * mu_ref[4:5, :]).astype(BF16)
    xa = (x + xx * mu_ref[5:6, :]).astype(BF16)
    hw = jnp.tanh(_mm(xw, w1_ref[...]))
    z = w0_ref[...] + _mm(hw.astype(BF16), w2_ref[...])
    sp = jnp.maximum(-z, 0.0) + jnp.log1p(jnp.exp(-jnp.abs(z)))
    lw = -jnp.exp(-sp - 0.5)
    ha = _mm(xa, a1_ref[...])
    za = a0_ref[...] + _mm(ha.astype(BF16), a2_ref[...])
    a = 1.0 / (1.0 + jnp.exp(-za))
    for q in range(lw_ref.shape[0]):
        lw_ref[q] = lw[:, q * SLAB:(q + 1) * SLAB]
        a_ref[q] = a[:, q * SLAB:(q + 1) * SLAB].astype(BF16)


def _rwkv_mix(x2d, mu, w0, w1, w2, a0, a1, a2, *, seq_len):
    bt, c = x2d.shape
    tm = min(256, seq_len)
    nq = c // SLAB
    kern = functools.partial(_rwkv_mix_kernel, seq_tiles=seq_len // tm)
    full = lambda shape: pl.BlockSpec(shape, lambda i: (0,) * len(shape))
    return pl.pallas_call(
        kern,
        grid=(bt // tm,),
        in_specs=[
            pl.BlockSpec((tm, c), lambda i: (i, 0)),
            pl.BlockSpec((8, c), lambda i: (jnp.maximum(i * (tm // 8) - 1, 0), 0)),
            full((6, c)),
            full((1, c)), full((c, LORA_PAD)), full((LORA_PAD, c)),
            full((1, c)), full((c, LORA_PAD)), full((LORA_PAD, c)),
        ],
        out_specs=[
            pl.BlockSpec((4, tm, c), lambda i: (0, i, 0)),
            pl.BlockSpec((nq, tm, SLAB), lambda i: (0, i, 0)),
            pl.BlockSpec((nq, tm, SLAB), lambda i: (0, i, 0)),
        ],
        out_shape=[
            jax.ShapeDtypeStruct((4, bt, c), BF16),
            jax.ShapeDtypeStruct((nq, bt, SLAB), F32),
            jax.ShapeDtypeStruct((nq, bt, SLAB), BF16),
        ],
        compiler_params=pltpu.CompilerParams(
            dimension_semantics=("arbitrary",),
            vmem_limit_bytes=VMEM_LIMIT),
    )(x2d, x2d, mu, w0, w1, w2, a0, a1, a2)


def _stream_matmul_kernel(x_ref, w_ref, o_ref):
    acc = _mm(x_ref[0], w_ref[0])
    for q in range(o_ref.shape[1]):
        o_ref[0, q] = acc[:, q * SLAB:(q + 1) * SLAB].astype(BF16)


def _stream_matmul(xs, w_in):
    ns, bt, c = xs.shape
    tm = min(1024, bt)
    tn = 512
    return pl.pallas_call(
        _stream_matmul_kernel,
        grid=(bt // tm, ns, c // tn),
        in_specs=[
            pl.BlockSpec((1, tm, c), lambda i, s, n: (s, i, 0)),
            pl.BlockSpec((1, c, tn), lambda i, s, n: (s, 0, n)),
        ],
        out_specs=pl.BlockSpec((1, tn // SLAB, tm, SLAB), lambda i, s, n: (s, n, i, 0)),
        out_shape=jax.ShapeDtypeStruct((ns, c // SLAB, bt, SLAB), BF16),
        compiler_params=pltpu.CompilerParams(
            dimension_semantics=("arbitrary", "arbitrary", "arbitrary"),
            vmem_limit_bytes=VMEM_LIMIT),
    )(xs, w_in)


def _wkv_masks():
    ti = lax.broadcasted_iota(jnp.int32, (PAIR, PAIR), 0)
    si = lax.broadcasted_iota(jnp.int32, (PAIR, PAIR), 1)
    same = (ti // WKV_CHUNK) == (si // WKV_CHUNK)
    tl = ti % WKV_CHUNK
    sl = si % WKV_CHUNK
    levels = []
    b = 1
    while b < WKV_CHUNK:
        levels.append(same & ((tl // b) == (sl // b) + 1) & (((tl // b) % 2) == 1))
        b *= 2
    return dict(same=same, eye=ti == si,
                strict=same & (sl < tl), incl=same & (sl <= tl),
                anti_strict=(~same) & (sl < tl), anti_incl=(~same) & (sl <= tl),
                levels=levels)


def _wkv_kernel(r_ref, k_ref, v_ref, g_ref, lw_ref, a_ref, kk_ref, ka_ref, rk_ref, gng_ref, gnb_ref,
                o_ref, h_ref):
    seq = o_ref.shape[1]
    ch = WKV_CHUNK
    gl = WKV_GROUP * ch
    masks = _wkv_masks()
    onesbd = jnp.where(masks["same"], 1.0, 0.0).astype(BF16)
    lane = lax.broadcasted_iota(jnp.int32, (ch, PAIR), 1)
    m0 = lane < RWKV_HEAD
    gt = lax.broadcasted_iota(jnp.int32, (gl, gl), 0)
    gs = lax.broadcasted_iota(jnp.int32, (gl, gl), 1)
    tri = jnp.where(((gt // ch) == (gs // ch)) & (gs <= gt), 1.0, 0.0).astype(BF16)

    def head_sum(x):
        return _mm(x.astype(BF16), onesbd)

    h_ref[...] = jnp.zeros(h_ref.shape, F32)

    npair = SLAB // PAIR
    inv_n = 1.0 / RWKV_HEAD

    def group(gi, carry):
        row0 = pl.multiple_of(gi * gl, gl)
        rows = pl.ds(row0, gl)
        pre = []
        for p in range(npair):
            lanes = slice(p * PAIR, (p + 1) * PAIR)
            r = r_ref[0, 0, rows, lanes].astype(F32)
            k = k_ref[0, 0, rows, lanes].astype(F32)
            v = v_ref[0, 0, rows, lanes].astype(F32)
            lw = lw_ref[0, rows, lanes]
            a = a_ref[0, rows, lanes].astype(F32)
            l1, l2 = _split2(lw)
            c = _mm(tri, l1) + _mm(tri, l2)
            pw = jnp.exp(c)
            pinv = jnp.exp(-c)
            pex = jnp.exp(c - lw)
            kkr = k * kk_ref[0, :, lanes]
            kk = kkr * lax.rsqrt(jnp.maximum(head_sum(kkr * kkr), 1e-24))
            kmod = k * (1.0 + (a - 1.0) * ka_ref[0, :, lanes])
            bonus = head_sum(r * kmod * rk_ref[0, :, lanes]) * v
            pre.append(dict(v=v, pw=pw, rt=r * pw, bt=kk * pex, kt=kmod * pinv, at=a * kk * pinv,
                            bonus=bonus))

        units = []
        for p in range(npair):
            d = pre[p]
            for j in range(WKV_GROUP):
                cs = slice(j * ch, (j + 1) * ch)
                units.append(dict(rt=d["rt"][cs], bt=d["bt"][cs], kt=d["kt"][cs], at=d["at"][cs],
                                  v=d["v"][cs], pl_row=d["pw"][j * ch + ch - 1:j * ch + ch, :]))

        z = jnp.zeros((ch, PAIR), F32)
        for u in units:
            rt_j, bt_j, kt_j, at_j = u["rt"], u["bt"], u["kt"], u["at"]
            lhs0 = jnp.concatenate([jnp.where(m0, rt_j, z), jnp.where(m0, bt_j, z)], 0).astype(BF16)
            lhs1 = jnp.concatenate([jnp.where(m0, z, rt_j), jnp.where(m0, z, bt_j)], 0).astype(BF16)
            rhs0 = jnp.concatenate([at_j, kt_j], 0).astype(BF16)
            rhs1 = jnp.concatenate([kt_j, at_j], 0).astype(BF16)
            u["a0"] = _mm_nt(lhs0, rhs0)
            u["a1"] = _mm_nt(lhs1, rhs1)
        for u in units:
            a0, a1 = u.pop("a0"), u.pop("a1")
            rblk = jnp.concatenate([a0[:ch], a1[:ch]], 0)
            bblk = jnp.concatenate([a0[ch:], a1[ch:]], 0)
            u["nmat"] = jnp.where(masks["strict"], bblk, 0.0)
            u["abk_anti"] = jnp.where(masks["anti_strict"], bblk, 0.0).astype(BF16)
            u["ara_bd"] = jnp.where(masks["incl"], rblk, 0.0).astype(BF16)
            u["ark_anti"] = jnp.where(masks["anti_incl"], rblk, 0.0).astype(BF16)
            v_j, bt_j = u["v"], u["bt"]
            u["v_sw"] = jnp.concatenate([jnp.where(m0, z, v_j), jnp.where(m0, v_j, z)], 0).astype(BF16)
            u["b_st"] = jnp.concatenate([jnp.where(m0, bt_j, z), jnp.where(m0, z, bt_j)], 0).astype(BF16)
        for u in units:
            u["w2"] = _mm(u["abk_anti"], u["v_sw"]).astype(BF16)

        eye = jnp.where(masks["eye"], 1.0, 0.0).astype(F32)
        for u in units:
            u["t"] = eye - jnp.where(masks["levels"][0], u["nmat"], 0.0)
        for li, lvl in enumerate(masks["levels"][1:]):
            blk = 2 << li
            if blk % 8 != 0:
                for u in units:
                    u["tb"] = u["t"].astype(BF16)
                    u["x"] = _mm(u["tb"], jnp.where(lvl, u["nmat"], 0.0).astype(BF16)).astype(BF16)
                for u in units:
                    u["t"] = u["t"] - _mm(u.pop("x"), u.pop("tb"))
            else:
                nblk = PAIR // blk
                for u in units:
                    u["tb"] = u["t"].astype(BF16)
                    t_odd = jnp.concatenate([u["t"][i * blk:(i + 1) * blk] for i in range(1, nblk, 2)], 0)
                    u["x"] = _mm(t_odd.astype(BF16), jnp.where(lvl, u["nmat"], 0.0).astype(BF16)).astype(BF16)
                for u in units:
                    y = _mm(u.pop("x"), u.pop("tb"))
                    t = u["t"]
                    u["t"] = jnp.concatenate(
                        [t[i * blk:(i + 1) * blk] - y[(i // 2) * blk:(i // 2 + 1) * blk] if i % 2
                         else t[i * blk:(i + 1) * blk] for i in range(nblk)], 0)

        for u in units:
            u["tbv"] = _mm(u["t"].astype(BF16), jnp.concatenate([u["b_st"], u["w2"]], 1))
        zero_st = jnp.zeros((PAIR, PAIR), BF16)
        for u in units:
            tbv = u["tbv"]
            bp_st = tbv[:, :PAIR]
            vp_st = tbv[:, PAIR:]
            rhs_y = jnp.concatenate([
                jnp.concatenate([u["v_sw"], zero_st], 1),
                jnp.concatenate([(-vp_st).astype(BF16), (-bp_st).astype(BF16)], 1)], 0)
            u["yr"] = _mm(jnp.concatenate([u["ark_anti"], u["ara_bd"]], 1), rhs_y)
            bp = bp_st[:ch] + bp_st[ch:]
            vp = vp_st[:ch] + vp_st[ch:]
            khat = u["kt"] * u["pl_row"]
            ahat = u["at"] * u["pl_row"]
            zc = jnp.zeros((ch, PAIR), BF16)
            lhs_t = jnp.concatenate([u["v"], -vp, -ahat], 0).astype(BF16)
            rhs_t = jnp.concatenate([
                jnp.concatenate([khat.astype(BF16), zc], 1),
                jnp.concatenate([ahat.astype(BF16), zc], 1),
                jnp.concatenate([zc, bp.astype(BF16)], 1)], 0)
            u["gm"] = _mm_tn(lhs_t, rhs_t)
        for u in units:
            yr, gm = u["yr"], u["gm"]
            u["y0"] = yr[:ch, :PAIR] + yr[ch:, :PAIR]
            u["rp"] = (u["rt"] + yr[:ch, PAIR:] + yr[ch:, PAIR:]).astype(BF16)
            u["g_t"] = jnp.where(masks["same"], gm[:, :PAIR], 0.0)
            u["m2"] = jnp.where(masks["same"], gm[:, PAIR:], 0.0).astype(BF16)

        hts = [h_ref[p] for p in range(npair)]
        ys = [[] for _ in range(npair)]
        for j in range(WKV_GROUP):
            for p in range(npair):
                u = units[p * WKV_GROUP + j]
                htb = hts[p].astype(BF16)
                ys[p].append(_mm_nt(u["rp"], htb) + u["y0"])
                hts[p] = hts[p] * u["pl_row"] + _mm_nt(htb, u["m2"]) + u["g_t"]
        for p in range(npair):
            h_ref[p] = hts[p]

        for p in range(npair):
            lanes = slice(p * PAIR, (p + 1) * PAIR)
            y = jnp.concatenate(ys[p], 0)
            mean = head_sum(y) * inv_n
            yc = y - mean
            var = head_sum(yc * yc) * inv_n
            yn = yc * lax.rsqrt(var + GN_EPS) * gng_ref[0, :, lanes] + gnb_ref[0, :, lanes]
            g = g_ref[0, 0, rows, lanes].astype(F32)
            out = (yn + pre[p]["bonus"]) * (g / (1.0 + jnp.exp(-g)))
            o_ref[0, rows, lanes] = out.astype(BF16)
        return carry

    lax.fori_loop(0, seq // gl, group, 0)


def _wkv(proj, lw, a, kk, ka, rk, gng, gnb, *, batch, seq_len):
    nq = lw.shape[0]
    bt = lw.shape[1]
    pspec = lambda s: pl.BlockSpec((1, 1, seq_len, SLAB), lambda b, q, s=s: (s, q, b, 0))
    aspec = pl.BlockSpec((1, seq_len, SLAB), lambda b, q: (q, b, 0))
    vspec = pl.BlockSpec((1, 1, SLAB), lambda b, q: (q, 0, 0))
    return pl.pallas_call(
        _wkv_kernel,
        grid=(batch, nq),
        in_specs=[pspec(0), pspec(1), pspec(2), pspec(3), aspec, aspec, vspec, vspec, vspec, vspec, vspec],
        out_specs=aspec,
        out_shape=jax.ShapeDtypeStruct((nq, bt, SLAB), BF16),
        scratch_shapes=[pltpu.VMEM((SLAB // PAIR, PAIR, PAIR), F32)],
        compiler_params=pltpu.CompilerParams(
            dimension_semantics=("arbitrary", "arbitrary"),
            vmem_limit_bytes=VMEM_LIMIT),
    )(proj, proj, proj, proj, lw, a, kk, ka, rk, gng, gnb)


def _out_ln_kernel(y_ref, x_ref, w_ref, g_ref, b_ref, *o_refs):
    acc = _mm(y_ref[0], w_ref[0])
    for q in range(1, y_ref.shape[0]):
        acc = acc + _mm(y_ref[q], w_ref[q])
    h = DEEPNORM_ALPHA * x_ref[...] + acc
    mu = jnp.mean(h, axis=-1, keepdims=True)
    hc = h - mu
    var = jnp.mean(hc * hc, axis=-1, keepdims=True)
    out = hc * lax.rsqrt(var + LN_EPS) * g_ref[...] + b_ref[...]
    o_refs[0][...] = out
    if len(o_refs) > 1:
        o_refs[1][...] = out.astype(BF16)


def _out_ln(y_slab, x2d, w_slab, ln_g, ln_b, *, with_bf16):
    nq, bt, _ = y_slab.shape
    c = x2d.shape[1]
    tm = min(256, bt)
    row = pl.BlockSpec((tm, c), lambda i: (i, 0))
    out_specs = [row]
    out_shape = [jax.ShapeDtypeStruct((bt, c), F32)]
    if with_bf16:
        out_specs.append(row)
        out_shape.append(jax.ShapeDtypeStruct((bt, c), BF16))
    return pl.pallas_call(
        _out_ln_kernel,
        grid=(bt // tm,),
        in_specs=[
            pl.BlockSpec((nq, tm, SLAB), lambda i: (0, i, 0)),
            row,
            pl.BlockSpec((nq, SLAB, c), lambda i: (0, 0, 0)),
            pl.BlockSpec((1, c), lambda i: (0, 0)),
            pl.BlockSpec((1, c), lambda i: (0, 0)),
        ],
        out_specs=out_specs,
        out_shape=out_shape,
        compiler_params=pltpu.CompilerParams(
            dimension_semantics=("arbitrary",),
            vmem_limit_bytes=VMEM_LIMIT),
    )(y_slab, x2d, w_slab, ln_g, ln_b)


def _slab_matmul_kernel(x_ref, w_ref, o_ref, *, q_tiles):
    acc = _mm(x_ref[...], w_ref[...])
    acc = acc * jnp.where(pl.program_id(1) < q_tiles, DIFF_HEAD_DIM ** -0.5, 1.0)
    for q in range(o_ref.shape[0]):
        o_ref[q] = acc[:, q * SLAB:(q + 1) * SLAB].astype(BF16)


def _slab_matmul(x_bf, w_bf, *, q_cols):
    bt, c = x_bf.shape
    n_out = w_bf.shape[1]
    tm = min(1024, bt)
    tn = 512
    return pl.pallas_call(
        functools.partial(_slab_matmul_kernel, q_tiles=q_cols // tn),
        grid=(bt // tm, n_out // tn),
        in_specs=[
            pl.BlockSpec((tm, c), lambda i, n: (i, 0)),
            pl.BlockSpec((c, tn), lambda i, n: (0, n)),
        ],
        out_specs=pl.BlockSpec((tn // SLAB, tm, SLAB), lambda i, n: (n, i, 0)),
        out_shape=jax.ShapeDtypeStruct((n_out // SLAB, bt, SLAB), BF16),
        compiler_params=pltpu.CompilerParams(
            dimension_semantics=("arbitrary", "arbitrary"),
            vmem_limit_bytes=VMEM_LIMIT),
    )(x_bf, w_bf)


def _attn_kernel(q_ref, k_ref, v_ref, g_ref, lam_ref, sg_ref, o_ref, m_ref, l_ref, acc_ref,
                 *, tq, lam_init):
    h = pl.program_id(1)
    qi = pl.program_id(2)
    dh = DIFF_HEAD_DIM
    slope = jnp.exp2(-(h + 1).astype(F32))
    lam_p = lam_ref[...]
    lam = (jnp.exp(jnp.sum(lam_p[0:1] * lam_p[1:2])) - jnp.exp(jnp.sum(lam_p[2:3] * lam_p[3:4]))
           + lam_init)

    ii = lax.broadcasted_iota(jnp.int32, (tq, tq), 0)
    jj = lax.broadcasted_iota(jnp.int32, (tq, tq), 1)
    dmat = (ii - jj).astype(F32)
    allowed = (jj // MASK_CHUNK) <= (ii // MASK_CHUNK)

    m_ref[...] = jnp.full(m_ref.shape, -jnp.inf, F32)
    l_ref[...] = jnp.zeros(l_ref.shape, F32)
    acc_ref[...] = jnp.zeros(acc_ref.shape, F32)

    nl = tq // 128

    def step(blocks):
        kbs, vbs, biases = [], [], []
        for j, masked in blocks:
            k0 = pl.multiple_of(j * tq, tq)
            if masked:
                biases.append(slope * ((k0 + ii).astype(F32) - jnp.abs(dmat)))
            else:
                biases.append(slope * (k0 + jj[0:1, :]).astype(F32))
            kbs.append(k_ref[0, pl.ds(k0, tq), :])
            vbs.append(v_ref[0, pl.ds(k0, tq), :])
        s = [[None] * len(blocks) for _ in range(2)]
        for u in range(2):
            lanes = slice(u * dh, (u + 1) * dh)
            for b, (j, masked) in enumerate(blocks):
                su = _mm_nt(q_ref[0, :, lanes], kbs[b][:, lanes]) + biases[b]
                if masked:
                    su = jnp.where(allowed, su, -jnp.inf)
                s[u][b] = su
        m_new, p = [], [[None] * len(blocks) for _ in range(2)]
        for u in range(2):
            pieces = [s[u][b][:, c * 128:(c + 1) * 128] for b in range(len(blocks)) for c in range(nl)]
            smax = functools.reduce(jnp.maximum, pieces)
            m_new.append(jnp.maximum(m_ref[u], jnp.max(smax, axis=1, keepdims=True)))
            mrep = jnp.concatenate([m_new[u]] * nl, axis=1)
            for b in range(len(blocks)):
                p[u][b] = jnp.exp(s[u][b] - mrep)
        pv = [sum(_mm(p[u][b].astype(BF16), vbs[b]) for b in range(len(blocks))) for u in range(2)]
        for u in range(2):
            pieces = [p[u][b][:, c * 128:(c + 1) * 128] for b in range(len(blocks)) for c in range(nl)]
            psum = jnp.sum(functools.reduce(jnp.add, pieces), axis=1, keepdims=True)
            corr = jnp.exp(m_ref[u] - m_new[u])
            l_ref[u] = corr * l_ref[u] + psum
            acc_ref[u] = jnp.concatenate([corr] * (SLAB // 128), axis=1) * acc_ref[u] + pv[u]
            m_ref[u] = m_new[u]

    def body(jp, carry):
        step([(2 * jp, False), (2 * jp + 1, False)])
        return carry

    lax.fori_loop(0, qi // 2, body, 0)

    @pl.when(qi % 2 == 1)
    def _():
        step([(qi - 1, False), (qi, True)])

    @pl.when(qi % 2 == 0)
    def _():
        step([(qi, True)])

    inv_l = [1.0 / l_ref[u][:, 0:1] for u in range(2)]
    o = acc_ref[0] * inv_l[0] - lam * (acc_ref[1] * inv_l[1])
    o = o * lax.rsqrt(jnp.mean(o * o, axis=-1, keepdims=True) + SUBLN_EPS) * sg_ref[...] * (1.0 - lam_init)
    g = g_ref[0].astype(F32)
    o_ref[0] = (o * (g / (1.0 + jnp.exp(-g)))).astype(BF16)


def _diff_attention(qgkv, lam_p, subln_g, *, batch, seq_len, layer):
    nq = qgkv.shape[0] // 4
    bt = qgkv.shape[1]
    tq = min(256, seq_len)
    nqb = seq_len // tq
    lam_init = 0.8 - 0.6 * math.exp(-0.3 * layer)
    kern = functools.partial(_attn_kernel, tq=tq, lam_init=lam_init)
    return pl.pallas_call(
        kern,
        grid=(batch, nq, nqb),
        in_specs=[
            pl.BlockSpec((1, tq, SLAB), lambda b, h, i: (h, b * nqb + i, 0)),
            pl.BlockSpec((1, seq_len, SLAB), lambda b, h, i: (2 * nq + h, b, 0)),
            pl.BlockSpec((1, seq_len, SLAB), lambda b, h, i: (3 * nq + h, b, 0)),
            pl.BlockSpec((1, tq, SLAB), lambda b, h, i: (nq + h, b * nqb + i, 0)),
            pl.BlockSpec(lam_p.shape, lambda b, h, i: (0, 0)),
            pl.BlockSpec((1, SLAB), lambda b, h, i: (0, 0)),
        ],
        out_specs=pl.BlockSpec((1, tq, SLAB), lambda b, h, i: (h, b * nqb + i, 0)),
        out_shape=jax.ShapeDtypeStruct((nq, bt, SLAB), BF16),
        scratch_shapes=[pltpu.VMEM((2, tq, 128), F32), pltpu.VMEM((2, tq, 128), F32),
                        pltpu.VMEM((2, tq, SLAB), F32)],
        compiler_params=pltpu.CompilerParams(
            dimension_semantics=("arbitrary", "arbitrary", "arbitrary"),
            vmem_limit_bytes=VMEM_LIMIT),
    )(qgkv, qgkv, qgkv, qgkv, lam_p, subln_g)


def _pad_cols(w):
    return jnp.pad(w, ((0, 0), (0, LORA_PAD - w.shape[1])))


def _pad_rows(w):
    return jnp.pad(w, ((0, LORA_PAD - w.shape[0]), (0, 0)))


def kernel(x, a_mu_proj, a_mu_lora, a_w_in, a_w0, a_w1, a_w2, a_a0, a_a1, a_a2, a_k_k, a_k_a, a_r_k,
           a_gn_g, a_gn_b, a_w_out, w_k_shared, w_v_shared, b_w_qg, b_lambda, b_subln_g, b_w_out,
           ln_g, ln_b):
    batch, seq_len, c = x.shape
    assert a_w_in.shape[0] == 1 and b_w_qg.shape[0] == 1 and ln_g.shape[0] == DEPTH
    assert c % SLAB == 0 and seq_len % (WKV_CHUNK * WKV_GROUP) == 0
    bt = batch * seq_len
    nq = c // SLAB
    x2d = x.reshape(bt, c)
    slab_vec = lambda p: p.reshape(nq, 1, SLAB)
    slab_rows = lambda w: w.astype(BF16).reshape(nq, SLAB, w.shape[1])

    mu = jnp.concatenate([a_mu_proj[0], a_mu_lora[0]], axis=0)
    xs, lw, a = _rwkv_mix(
        x2d, mu,
        a_w0[0][None], _pad_cols(a_w1[0]).astype(BF16), _pad_rows(a_w2[0]).astype(BF16),
        a_a0[0][None], _pad_cols(a_a1[0]).astype(BF16), _pad_rows(a_a2[0]).astype(BF16),
        seq_len=seq_len)
    proj = _stream_matmul(xs, a_w_in[0].astype(BF16))
    yg = _wkv(proj, lw, a, slab_vec(a_k_k[0]), slab_vec(a_k_a[0]), slab_vec(a_r_k[0]),
              slab_vec(a_gn_g[0]), slab_vec(a_gn_b[0]), batch=batch, seq_len=seq_len)
    x1, x1_bf = _out_ln(yg, x2d, slab_rows(a_w_out[0]), ln_g[0][None], ln_b[0][None], with_bf16=True)

    w_all = jnp.concatenate([b_w_qg[0], w_k_shared, w_v_shared], axis=1).astype(BF16)
    qgkv = _slab_matmul(x1_bf, w_all, q_cols=c)
    og = _diff_attention(qgkv, b_lambda[0], b_subln_g[0][None], batch=batch, seq_len=seq_len, layer=1)
    (out,) = _out_ln(og, x1, slab_rows(b_w_out[0]), ln_g[1][None], ln_b[1][None], with_bf16=False)
    return out.reshape(batch, seq_len, c)
```

```python
import functools
import math

import jax
import jax.numpy as jnp
from jax import lax
from jax.experimental import pallas as pl
from jax.experimental.pallas import tpu as pltpu

F32 = jnp.float32
BF16 = jnp.bfloat16

DEPTH = 2
RWKV_HEAD = 64
MASK_CHUNK = 64
DIFF_HEAD_DIM = 128
GN_EPS = 64e-5
SUBLN_EPS = 1e-5
LN_EPS = 1e-5
DEEPNORM_ALPHA = (2.0 * DEPTH) ** 0.25

SLAB = 256
PAIR = 128
LORA_PAD = 128
WKV_CHUNK = 64
WKV_GROUP = 4
WKV_SLABS = 2
ATTN_HEADS = 2
VMEM_LIMIT = 56 * 1024 * 1024


def _mm(a, b):
    return jnp.dot(a, b, preferred_element_type=F32)


def _mm_nt(a, b):
    return lax.dot_general(a, b, (((1,), (1,)), ((), ())), preferred_element_type=F32)


def _mm_tn(a, b):
    return lax.dot_general(a, b, (((0,), (0,)), ((), ())), preferred_element_type=F32)


def _split2(x):
    hi = x.astype(BF16)
    lo = (x - hi.astype(F32)).astype(BF16)
    return hi, lo


def _rwkv_mix_kernel(x_ref, xp_ref, mu_ref, w0_ref, w1_ref, w2_ref, a0_ref, a1_ref, a2_ref,
                     xs_ref, lw_ref, a_ref, *, seq_tiles):
    i = pl.program_id(0)
    x = x_ref[...]
    prev_last = xp_ref[7:8, :]
    prev_last = jnp.where((i % seq_tiles) == 0, 0.0, prev_last)
    row = lax.broadcasted_iota(jnp.int32, x.shape, 0)
    xsh = jnp.where(row == 0, prev_last, pltpu.roll(x, 1, axis=0))
    xx = xsh - x
    for st in range(4):
        xs_ref[st] = (x + xx * mu_ref[st:st + 1, :]).astype(BF16)
    xw = (x + xx * mu_ref[4:5, :]).astype(BF16)
    xa = (x + xx * mu_ref[5:6, :]).astype(BF16)
    hw = jnp.tanh(_mm(xw, w1_ref[...]))
    z = w0_ref[...] + _mm(hw.astype(BF16), w2_ref[...])
    sp = jnp.maximum(-z, 0.0) + jnp.log1p(jnp.exp(-jnp.abs(z)))
    lw = -jnp.exp(-sp - 0.5)
    ha = _mm(xa, a1_ref[...])
    za = a0_ref[...] + _mm(ha.astype(BF16), a2_ref[...])
    a = 1.0 / (1.0 + jnp.exp(-za))
    for q in range(lw_ref.shape[0]):
        lw_ref[q] = lw[:, q * SLAB:(q + 1) * SLAB]
        a_ref[q] = a[:, q * SLAB:(q + 1) * SLAB].astype(BF16)


def _rwkv_mix(x2d, mu, w0, w1, w2, a0, a1, a2, *, seq_len):
    bt, c = x2d.shape
    tm = min(256, seq_len)
    nq = c // SLAB
    kern = functools.partial(_rwkv_mix_kernel, seq_tiles=seq_len // tm)
    full = lambda shape: pl.BlockSpec(shape, lambda i: (0,) * len(shape))
    return pl.pallas_call(
        kern,
        grid=(bt // tm,),
        in_specs=[
            pl.BlockSpec((tm, c), lambda i: (i, 0)),
            pl.BlockSpec((8, c), lambda i: (jnp.maximum(i * (tm // 8) - 1, 0), 0)),
            full((6, c)),
            full((1, c)), full((c, LORA_PAD)), full((LORA_PAD, c)),
            full((1, c)), full((c, LORA_PAD)), full((LORA_PAD, c)),
        ],
        out_specs=[
            pl.BlockSpec((4, tm, c), lambda i: (0, i, 0)),
            pl.BlockSpec((nq, tm, SLAB), lambda i: (0, i, 0)),
            pl.BlockSpec((nq, tm, SLAB), lambda i: (0, i, 0)),
        ],
        out_shape=[
            jax.ShapeDtypeStruct((4, bt, c), BF16),
            jax.ShapeDtypeStruct((nq, bt, SLAB), F32),
            jax.ShapeDtypeStruct((nq, bt, SLAB), BF16),
        ],
        compiler_params=pltpu.CompilerParams(
            dimension_semantics=("arbitrary",),
            vmem_limit_bytes=VMEM_LIMIT),
    )(x2d, x2d, mu, w0, w1, w2, a0, a1, a2)


def _stream_matmul_kernel(x_ref, w_ref, o_ref):
    acc = _mm(x_ref[0], w_ref[0])
    for q in range(o_ref.shape[1]):
        o_ref[0, q] = acc[:, q * SLAB:(q + 1) * SLAB].astype(BF16)


def _stream_matmul(xs, w_in):
    ns, bt, c = xs.shape
    tm = min(1024, bt)
    tn = 512
    return pl.pallas_call(
        _stream_matmul_kernel,
        grid=(bt // tm, ns, c // tn),
        in_specs=[
            pl.BlockSpec((1, tm, c), lambda i, s, n: (s, i, 0)),
            pl.BlockSpec((1, c, tn), lambda i, s, n: (s, 0, n)),
        ],
        out_specs=pl.BlockSpec((1, tn // SLAB, tm, SLAB), lambda i, s, n: (s, n, i, 0)),
        out_shape=jax.ShapeDtypeStruct((ns, c // SLAB, bt, SLAB), BF16),
        compiler_params=pltpu.CompilerParams(
            dimension_semantics=("arbitrary", "arbitrary", "arbitrary"),
            vmem_limit_bytes=VMEM_LIMIT),
    )(xs, w_in)


def _wkv_masks():
    ti = lax.broadcasted_iota(jnp.int32, (PAIR, PAIR), 0)
    si = lax.broadcasted_iota(jnp.int32, (PAIR, PAIR), 1)
    same = (ti // WKV_CHUNK) == (si // WKV_CHUNK)
    tl = ti % WKV_CHUNK
    sl = si % WKV_CHUNK
    levels = []
    b = 1
    while b < WKV_CHUNK:
        levels.append(same & ((tl // b) == (sl // b) + 1) & (((tl // b) % 2) == 1))
        b *= 2
    return dict(same=same, eye=ti == si,
                strict=same & (sl < tl), incl=same & (sl <= tl),
                anti_strict=(~same) & (sl < tl), anti_incl=(~same) & (sl <= tl),
                levels=levels)


def _wkv_kernel(r_ref, k_ref, v_ref, g_ref, lw_ref, a_ref, kk_ref, ka_ref, rk_ref, gng_ref, gnb_ref,
                o_ref, h_ref):
    seq = o_ref.shape[1]
    ch = WKV_CHUNK
    gl = WKV_GROUP * ch
    masks = _wkv_masks()
    onesbd = jnp.where(masks["same"], 1.0, 0.0).astype(BF16)
    lane = lax.broadcasted_iota(jnp.int32, (ch, PAIR), 1)
    m0 = lane < RWKV_HEAD
    gt = lax.broadcasted_iota(jnp.int32, (gl, gl), 0)
    gs = lax.broadcasted_iota(jnp.int32, (gl, gl), 1)
    tri = jnp.where(((gt // ch) == (gs // ch)) & (gs <= gt), 1.0, 0.0).astype(BF16)

    def head_sum(x):
        return _mm(x.astype(BF16), onesbd)

    h_ref[...] = jnp.zeros(h_ref.shape, F32)

    pps = SLAB // PAIR
    npair = o_ref.shape[0] * pps
    inv_n = 1.0 / RWKV_HEAD

    def group(gi, carry):
        row0 = pl.multiple_of(gi * gl, gl)
        rows = pl.ds(row0, gl)
        pre = []
        for p in range(npair):
            sb = p // pps
            lanes = slice((p % pps) * PAIR, (p % pps + 1) * PAIR)
            r = r_ref[0, sb, rows, lanes].astype(F32)
            k = k_ref[0, sb, rows, lanes].astype(F32)
            v = v_ref[0, sb, rows, lanes].astype(F32)
            lw = lw_ref[sb, rows, lanes]
            a = a_ref[sb, rows, lanes].astype(F32)
            l1, l2 = _split2(lw)
            c = _mm(tri, l1) + _mm(tri, l2)
            pw = jnp.exp(c)
            pinv = jnp.exp(-c)
            pex = jnp.exp(c - lw)
            kkr = k * kk_ref[sb, :, lanes]
            kk = kkr * lax.rsqrt(jnp.maximum(head_sum(kkr * kkr), 1e-24))
            kmod = k * (1.0 + (a - 1.0) * ka_ref[sb, :, lanes])
            bonus = head_sum(r * kmod * rk_ref[sb, :, lanes]) * v
            pre.append(dict(v=v, pw=pw, rt=r * pw, bt=kk * pex, kt=kmod * pinv, at=a * kk * pinv,
                            bonus=bonus))

        units = []
        for p in range(npair):
            d = pre[p]
            for j in range(WKV_GROUP):
                cs = slice(j * ch, (j + 1) * ch)
                units.append(dict(rt=d["rt"][cs], bt=d["bt"][cs], kt=d["kt"][cs], at=d["at"][cs],
                                  v=d["v"][cs], pl_row=d["pw"][j * ch + ch - 1:j * ch + ch, :]))

        z = jnp.zeros((ch, PAIR), F32)
        for u in units:
            rt_j, bt_j, kt_j, at_j = u["rt"], u["bt"], u["kt"], u["at"]
            lhs0 = jnp.concatenate([jnp.where(m0, rt_j, z), jnp.where(m0, bt_j, z)], 0).astype(BF16)
            lhs1 = jnp.concatenate([jnp.where(m0, z, rt_j), jnp.where(m0, z, bt_j)], 0).astype(BF16)
            rhs0 = jnp.concatenate([at_j, kt_j], 0).astype(BF16)
            rhs1 = jnp.concatenate([kt_j, at_j], 0).astype(BF16)
            u["a0"] = _mm_nt(lhs0, rhs0)
            u["a1"] = _mm_nt(lhs1, rhs1)
        for u in units:
            a0, a1 = u.pop("a0"), u.pop("a1")
            rblk = jnp.concatenate([a0[:ch], a1[:ch]], 0)
            bblk = jnp.concatenate([a0[ch:], a1[ch:]], 0)
            u["nmat"] = jnp.where(masks["strict"], bblk, 0.0)
            u["abk_anti"] = jnp.where(masks["anti_strict"], bblk, 0.0).astype(BF16)
            u["ara_bd"] = jnp.where(masks["incl"], rblk, 0.0).astype(BF16)
            u["ark_anti"] = jnp.where(masks["anti_incl"], rblk, 0.0).astype(BF16)
            v_j, bt_j = u["v"], u["bt"]
            u["v_sw"] = jnp.concatenate([jnp.where(m0, z, v_j), jnp.where(m0, v_j, z)], 0).astype(BF16)
            u["b_st"] = jnp.concatenate([jnp.where(m0, bt_j, z), jnp.where(m0, z, bt_j)], 0).astype(BF16)
        for u in units:
            u["w2"] = _mm(u["abk_anti"], u["v_sw"]).astype(BF16)

        eye = jnp.where(masks["eye"], 1.0, 0.0).astype(F32)
        for u in units:
            u["t"] = eye - jnp.where(masks["levels"][0], u["nmat"], 0.0)
        for li, lvl in enumerate(masks["levels"][1:]):
            blk = 2 << li
            if blk % 8 != 0:
                for u in units:
                    u["tb"] = u["t"].astype(BF16)
                    u["x"] = _mm(u["tb"], jnp.where(lvl, u["nmat"], 0.0).astype(BF16)).astype(BF16)
                for u in units:
                    u["t"] = u["t"] - _mm(u.pop("x"), u.pop("tb"))
            else:
                nblk = PAIR // blk
                for u in units:
                    u["tb"] = u["t"].astype(BF16)
                    t_odd = jnp.concatenate([u["t"][i * blk:(i + 1) * blk] for i in range(1, nblk, 2)], 0)
                    u["x"] = _mm(t_odd.astype(BF16), jnp.where(lvl, u["nmat"], 0.0).astype(BF16)).astype(BF16)
                for u in units:
                    y = _mm(u.pop("x"), u.pop("tb"))
                    t = u["t"]
                    u["t"] = jnp.concatenate(
                        [t[i * blk:(i + 1) * blk] - y[(i // 2) * blk:(i // 2 + 1) * blk] if i % 2
                         else t[i * blk:(i + 1) * blk] for i in range(nblk)], 0)

        for u in units:
            u["tbv"] = _mm(u["t"].astype(BF16), jnp.concatenate([u["b_st"], u["w2"]], 1))
        zero_st = jnp.zeros((PAIR, PAIR), BF16)
        for u in units:
            tbv = u["tbv"]
            bp_st = tbv[:, :PAIR]
            vp_st = tbv[:, PAIR:]
            rhs_y = jnp.concatenate([
                jnp.concatenate([u["v_sw"], zero_st], 1),
                jnp.concatenate([(-vp_st).astype(BF16), (-bp_st).astype(BF16)], 1)], 0)
            u["yr"] = _mm(jnp.concatenate([u["ark_anti"], u["ara_bd"]], 1), rhs_y)
            bp = bp_st[:ch] + bp_st[ch:]
            vp = vp_st[:ch] + vp_st[ch:]
            khat = u["kt"] * u["pl_row"]
            ahat = u["at"] * u["pl_row"]
            zc = jnp.zeros((ch, PAIR), BF16)
            lhs_t = jnp.concatenate([u["v"], -vp, -ahat], 0).astype(BF16)
            rhs_t = jnp.concatenate([
                jnp.concatenate([khat.astype(BF16), zc], 1),
                jnp.concatenate([ahat.astype(BF16), zc], 1),
                jnp.concatenate([zc, bp.astype(BF16)], 1)], 0)
            u["gm"] = _mm_tn(lhs_t, rhs_t)
        for u in units:
            yr, gm = u["yr"], u["gm"]
            u["y0"] = yr[:ch, :PAIR] + yr[ch:, :PAIR]
            u["rp"] = (u["rt"] + yr[:ch, PAIR:] + yr[ch:, PAIR:]).astype(BF16)
            u["g_t"] = jnp.where(masks["same"], gm[:, :PAIR], 0.0)
            u["m2"] = jnp.where(masks["same"], gm[:, PAIR:], 0.0).astype(BF16)

        hts = [h_ref[p] for p in range(npair)]
        ys = [[] for _ in range(npair)]
        for j in range(WKV_GROUP):
            for p in range(npair):
                u = units[p * WKV_GROUP + j]
                htb = hts[p].astype(BF16)
                ys[p].append(_mm_nt(u["rp"], htb) + u["y0"])
                hts[p] = hts[p] * u["pl_row"] + _mm_nt(htb, u["m2"]) + u["g_t"]
        for p in range(npair):
            h_ref[p] = hts[p]

        for p in range(npair):
            sb = p // pps
            lanes = slice((p % pps) * PAIR, (p % pps + 1) * PAIR)
            y = jnp.concatenate(ys[p], 0)
            mean = head_sum(y) * inv_n
            yc = y - mean
            var = head_sum(yc * yc) * inv_n
            yn = yc * lax.rsqrt(var + GN_EPS) * gng_ref[sb, :, lanes] + gnb_ref[sb, :, lanes]
            g = g_ref[0, sb, rows, lanes].astype(F32)
            out = (yn + pre[p]["bonus"]) * (g / (1.0 + jnp.exp(-g)))
            o_ref[sb, rows, lanes] = out.astype(BF16)
        return carry

    lax.fori_loop(0, seq // gl, group, 0)


def _wkv(proj, lw, a, kk, ka, rk, gng, gnb, *, batch, seq_len):
    nq = lw.shape[0]
    bt = lw.shape[1]
    ns = WKV_SLABS
    pspec = lambda s: pl.BlockSpec((1, ns, seq_len, SLAB), lambda b, q, s=s: (s, q, b, 0))
    aspec = pl.BlockSpec((ns, seq_len, SLAB), lambda b, q: (q, b, 0))
    vspec = pl.BlockSpec((ns, 1, SLAB), lambda b, q: (q, 0, 0))
    return pl.pallas_call(
        _wkv_kernel,
        grid=(batch, nq // ns),
        in_specs=[pspec(0), pspec(1), pspec(2), pspec(3), aspec, aspec, vspec, vspec, vspec, vspec, vspec],
        out_specs=aspec,
        out_shape=jax.ShapeDtypeStruct((nq, bt, SLAB), BF16),
        scratch_shapes=[pltpu.VMEM((ns * SLAB // PAIR, PAIR, PAIR), F32)],
        compiler_params=pltpu.CompilerParams(
            dimension_semantics=("arbitrary", "arbitrary"),
            vmem_limit_bytes=VMEM_LIMIT),
    )(proj, proj, proj, proj, lw, a, kk, ka, rk, gng, gnb)


def _out_ln_kernel(y_ref, x_ref, w_ref, g_ref, b_ref, *o_refs):
    acc = _mm(y_ref[0], w_ref[0])
    for q in range(1, y_ref.shape[0]):
        acc = acc + _mm(y_ref[q], w_ref[q])
    h = DEEPNORM_ALPHA * x_ref[...] + acc
    mu = jnp.mean(h, axis=-1, keepdims=True)
    hc = h - mu
    var = jnp.mean(hc * hc, axis=-1, keepdims=True)
    out = hc * lax.rsqrt(var + LN_EPS) * g_ref[...] + b_ref[...]
    o_refs[0][...] = out
    if len(o_refs) > 1:
        o_refs[1][...] = out.astype(BF16)


def _out_ln(y_slab, x2d, w_slab, ln_g, ln_b, *, with_bf16):
    nq, bt, _ = y_slab.shape
    c = x2d.shape[1]
    tm = min(256, bt)
    row = pl.BlockSpec((tm, c), lambda i: (i, 0))
    out_specs = [row]
    out_shape = [jax.ShapeDtypeStruct((bt, c), F32)]
    if with_bf16:
        out_specs.append(row)
        out_shape.append(jax.ShapeDtypeStruct((bt, c), BF16))
    return pl.pallas_call(
        _out_ln_kernel,
        grid=(bt // tm,),
        in_specs=[
            pl.BlockSpec((nq, tm, SLAB), lambda i: (0, i, 0)),
            row,
            pl.BlockSpec((nq, SLAB, c), lambda i: (0, 0, 0)),
            pl.BlockSpec((1, c), lambda i: (0, 0)),
            pl.BlockSpec((1, c), lambda i: (0, 0)),
        ],
        out_specs=out_specs,
        out_shape=out_shape,
        compiler_params=pltpu.CompilerParams(
            dimension_semantics=("arbitrary",),
            vmem_limit_bytes=VMEM_LIMIT),
    )(y_slab, x2d, w_slab, ln_g, ln_b)


def _slab_matmul_kernel(x_ref, w_ref, o_ref, *, q_tiles):
    acc = _mm(x_ref[...], w_ref[...])
    acc = acc * jnp.where(pl.program_id(1) < q_tiles, DIFF_HEAD_DIM ** -0.5, 1.0)
    for q in range(o_ref.shape[0]):
        o_ref[q] = acc[:, q * SLAB:(q + 1) * SLAB].astype(BF16)


def _slab_matmul(x_bf, w_bf, *, q_cols):
    bt, c = x_bf.shape
    n_out = w_bf.shape[1]
    tm = min(1024, bt)
    tn = 512
    return pl.pallas_call(
        functools.partial(_slab_matmul_kernel, q_tiles=q_cols // tn),
        grid=(bt // tm, n_out // tn),
        in_specs=[
            pl.BlockSpec((tm, c), lambda i, n: (i, 0)),
            pl.BlockSpec((c, tn), lambda i, n: (0, n)),
        ],
        out_specs=pl.BlockSpec((tn // SLAB, tm, SLAB), lambda i, n: (n, i, 0)),
        out_shape=jax.ShapeDtypeStruct((n_out // SLAB, bt, SLAB), BF16),
        compiler_params=pltpu.CompilerParams(
            dimension_semantics=("arbitrary", "arbitrary"),
            vmem_limit_bytes=VMEM_LIMIT),
    )(x_bf, w_bf)


def _attn_kernel(q_ref, k_ref, v_ref, g_ref, lam_ref, sg_ref, o_ref, m_ref, l_ref, acc_ref,
                 *, tq, lam_init):
    nh = q_ref.shape[0]
    hb = pl.program_id(1)
    qi = pl.program_id(2)
    dh = DIFF_HEAD_DIM
    slopes = [jnp.exp2(-(hb * nh + hh + 1).astype(F32)) for hh in range(nh)]
    lam_p = lam_ref[...]
    lam = (jnp.exp(jnp.sum(lam_p[0:1] * lam_p[1:2])) - jnp.exp(jnp.sum(lam_p[2:3] * lam_p[3:4]))
           + lam_init)

    ii = lax.broadcasted_iota(jnp.int32, (tq, tq), 0)
    jj = lax.broadcasted_iota(jnp.int32, (tq, tq), 1)
    dmat = (ii - jj).astype(F32)
    allowed = (jj // MASK_CHUNK) <= (ii // MASK_CHUNK)

    m_ref[...] = jnp.full(m_ref.shape, -jnp.inf, F32)
    l_ref[...] = jnp.zeros(l_ref.shape, F32)
    acc_ref[...] = jnp.zeros(acc_ref.shape, F32)

    nl = tq // 128
    streams = [(hh, u) for hh in range(nh) for u in range(2)]

    def step(blocks):
        nb = len(blocks)
        kbs = [[None] * nb for _ in range(nh)]
        vbs = [[None] * nb for _ in range(nh)]
        biases = [[None] * nb for _ in range(nh)]
        for b, (j, masked) in enumerate(blocks):
            k0 = pl.multiple_of(j * tq, tq)
            base = (k0 + ii).astype(F32) - jnp.abs(dmat) if masked else (k0 + jj[0:1, :]).astype(F32)
            for hh in range(nh):
                biases[hh][b] = slopes[hh] * base
                kbs[hh][b] = k_ref[hh, pl.ds(k0, tq), :]
                vbs[hh][b] = v_ref[hh, pl.ds(k0, tq), :]
        s = [[None] * nb for _ in streams]
        for w, (hh, u) in enumerate(streams):
            lanes = slice(u * dh, (u + 1) * dh)
            for b, (j, masked) in enumerate(blocks):
                su = _mm_nt(q_ref[hh, :, lanes], kbs[hh][b][:, lanes]) + biases[hh][b]
                if masked:
                    su = jnp.where(allowed, su, -jnp.inf)
                s[w][b] = su
        m_new, p = [], [[None] * nb for _ in streams]
        for w in range(len(streams)):
            pieces = [s[w][b][:, c * 128:(c + 1) * 128] for b in range(nb) for c in range(nl)]
            smax = functools.reduce(jnp.maximum, pieces)
            m_new.append(jnp.maximum(m_ref[w], jnp.max(smax, axis=1, keepdims=True)))
            mrep = jnp.concatenate([m_new[w]] * nl, axis=1)
            for b in range(nb):
                p[w][b] = jnp.exp(s[w][b] - mrep)
        pv = [sum(_mm(p[w][b].astype(BF16), vbs[hh][b]) for b in range(nb))
              for w, (hh, u) in enumerate(streams)]
        for w in range(len(streams)):
            pieces = [p[w][b][:, c * 128:(c + 1) * 128] for b in range(nb) for c in range(nl)]
            psum = jnp.sum(functools.reduce(jnp.add, pieces), axis=1, keepdims=True)
            corr = jnp.exp(m_ref[w] - m_new[w])
            l_ref[w] = corr * l_ref[w] + psum
            acc_ref[w] = jnp.concatenate([corr] * (SLAB // 128), axis=1) * acc_ref[w] + pv[w]
            m_ref[w] = m_new[w]

    def body(jp, carry):
        step([(2 * jp, False), (2 * jp + 1, False)])
        return carry

    lax.fori_loop(0, qi // 2, body, 0)

    @pl.when(qi % 2 == 1)
    def _():
        step([(qi - 1, False), (qi, True)])

    @pl.when(qi % 2 == 0)
    def _():
        step([(qi, True)])

    for hh in range(nh):
        inv_l = [1.0 / l_ref[2 * hh + u][:, 0:1] for u in range(2)]
        o = acc_ref[2 * hh] * inv_l[0] - lam * (acc_ref[2 * hh + 1] * inv_l[1])
        o = (o * lax.rsqrt(jnp.mean(o * o, axis=-1, keepdims=True) + SUBLN_EPS) * sg_ref[...]
             * (1.0 - lam_init))
        g = g_ref[hh].astype(F32)
        o_ref[hh] = (o * (g / (1.0 + jnp.exp(-g)))).astype(BF16)


def _diff_attention(qgkv, lam_p, subln_g, *, batch, seq_len, layer):
    nq = qgkv.shape[0] // 4
    bt = qgkv.shape[1]
    tq = min(256, seq_len)
    nh = ATTN_HEADS
    nhb = nq // nh
    nqb = seq_len // tq
    lam_init = 0.8 - 0.6 * math.exp(-0.3 * layer)
    kern = functools.partial(_attn_kernel, tq=tq, lam_init=lam_init)
    return pl.pallas_call(
        kern,
        grid=(batch, nhb, nqb),
        in_specs=[
            pl.BlockSpec((nh, tq, SLAB), lambda b, h, i: (h, b * nqb + i, 0)),
            pl.BlockSpec((nh, seq_len, SLAB), lambda b, h, i: (2 * nhb + h, b, 0)),
            pl.BlockSpec((nh, seq_len, SLAB), lambda b, h, i: (3 * nhb + h, b, 0)),
            pl.BlockSpec((nh, tq, SLAB), lambda b, h, i: (nhb + h, b * nqb + i, 0)),
            pl.BlockSpec(lam_p.shape, lambda b, h, i: (0, 0)),
            pl.BlockSpec((1, SLAB), lambda b, h, i: (0, 0)),
        ],
        out_specs=pl.BlockSpec((nh, tq, SLAB), lambda b, h, i: (h, b * nqb + i, 0)),
        out_shape=jax.ShapeDtypeStruct((nq, bt, SLAB), BF16),
        scratch_shapes=[pltpu.VMEM((2 * nh, tq, 128), F32), pltpu.VMEM((2 * nh, tq, 128), F32),
                        pltpu.VMEM((2 * nh, tq, SLAB), F32)],
        compiler_params=pltpu.CompilerParams(
            dimension_semantics=("arbitrary", "arbitrary", "arbitrary"),
            vmem_limit_bytes=VMEM_LIMIT),
    )(qgkv, qgkv, qgkv, qgkv, lam_p, subln_g)


def _pad_cols(w):
    return jnp.pad(w, ((0, 0), (0, LORA_PAD - w.shape[1])))


def _pad_rows(w):
    return jnp.pad(w, ((0, LORA_PAD - w.shape[0]), (0, 0)))


def kernel(x, a_mu_proj, a_mu_lora, a_w_in, a_w0, a_w1, a_w2, a_a0, a_a1, a_a2, a_k_k, a_k_a, a_r_k,
           a_gn_g, a_gn_b, a_w_out, w_k_shared, w_v_shared, b_w_qg, b_lambda, b_subln_g, b_w_out,
           ln_g, ln_b):
    batch, seq_len, c = x.shape
    assert a_w_in.shape[0] == 1 and b_w_qg.shape[0] == 1 and ln_g.shape[0] == DEPTH
    assert c % SLAB == 0 and seq_len % (WKV_CHUNK * WKV_GROUP) == 0
    bt = batch * seq_len
    nq = c // SLAB
    x2d = x.reshape(bt, c)
    slab_vec = lambda p: p.reshape(nq, 1, SLAB)
    slab_rows = lambda w: w.astype(BF16).reshape(nq, SLAB, w.shape[1])

    mu = jnp.concatenate([a_mu_proj[0], a_mu_lora[0]], axis=0)
    xs, lw, a = _rwkv_mix(
        x2d, mu,
        a_w0[0][None], _pad_cols(a_w1[0]).astype(BF16), _pad_rows(a_w2[0]).astype(BF16),
        a_a0[0][None], _pad_cols(a_a1[0]).astype(BF16), _pad_rows(a_a2[0]).astype(BF16),
        seq_len=seq_len)
    proj = _stream_matmul(xs, a_w_in[0].astype(BF16))
    yg = _wkv(proj, lw, a, slab_vec(a_k_k[0]), slab_vec(a_k_a[0]), slab_vec(a_r_k[0]),
              slab_vec(a_gn_g[0]), slab_vec(a_gn_b[0]), batch=batch, seq_len=seq_len)
    x1, x1_bf = _out_ln(yg, x2d, slab_rows(a_w_out[0]), ln_g[0][None], ln_b[0][None], with_bf16=True)

    w_all = jnp.concatenate([b_w_qg[0], w_k_shared, w_v_shared], axis=1).astype(BF16)
    qgkv = _slab_matmul(x1_bf, w_all, q_cols=c)
    og = _diff_attention(qgkv, b_lambda[0], b_subln_g[0][None], batch=batch, seq_len=seq_len, layer=1)
    (out,) = _out_ln(og, x1, slab_rows(b_w_out[0]), ln_g[1][None], ln_b[1][None], with_bf16=False)
    return out.reshape(batch, seq_len, c)
```

```python
import functools
import math

import jax
import jax.numpy as jnp
from jax import lax
from jax.experimental import pallas as pl
from jax.experimental.pallas import tpu as pltpu

F32 = jnp.float32
BF16 = jnp.bfloat16

DEPTH = 2
RWKV_HEAD = 64
MASK_CHUNK = 64
DIFF_HEAD_DIM = 128
GN_EPS = 64e-5
SUBLN_EPS = 1e-5
LN_EPS = 1e-5
DEEPNORM_ALPHA = (2.0 * DEPTH) ** 0.25

SLAB = 256
PAIR = 128
LORA_PAD = 128
WKV_CHUNK = 64
WKV_GROUP = 4
WKV_SLABS = 2
WKV_TIME_BLOCK = 2048
ATTN_HEADS = 4
VMEM_LIMIT = 56 * 1024 * 1024


def _mm(a, b):
    return jnp.dot(a, b, preferred_element_type=F32)


def _mm_nt(a, b):
    return lax.dot_general(a, b, (((1,), (1,)), ((), ())), preferred_element_type=F32)


def _mm_tn(a, b):
    return lax.dot_general(a, b, (((0,), (0,)), ((), ())), preferred_element_type=F32)


def _split2(x):
    hi = x.astype(BF16)
    lo = (x - hi.astype(F32)).astype(BF16)
    return hi, lo


def _rwkv_mix_kernel(x_ref, xp_ref, mu_ref, w0_ref, w1_ref, w2_ref, a0_ref, a1_ref, a2_ref,
                     xs_ref, lw_ref, a_ref, *, seq_tiles):
    i = pl.program_id(0)
    x = x_ref[...]
    prev_last = xp_ref[7:8, :]
    prev_last = jnp.where((i % seq_tiles) == 0, 0.0, prev_last)
    row = lax.broadcasted_iota(jnp.int32, x.shape, 0)
    xsh = jnp.where(row == 0, prev_last, pltpu.roll(x, 1, axis=0))
    xx = xsh - x
    for st in range(4):
        xs_ref[st] = (x + xx * mu_ref[st:st + 1, :]).astype(BF16)
    xw = (x + xx * mu_ref[4:5, :]).astype(BF16)
    xa = (x + xx * mu_ref[5:6, :]).astype(BF16)
    hw = jnp.tanh(_mm(xw, w1_ref[...]))
    z = w0_ref[...] + _mm(hw.astype(BF16), w2_ref[...])
    sp = jnp.maximum(-z, 0.0) + jnp.log1p(jnp.exp(-jnp.abs(z)))
    lw = -jnp.exp(-sp - 0.5)
    ha = _mm(xa, a1_ref[...])
    za = a0_ref[...] + _mm(ha.astype(BF16), a2_ref[...])
    a = 1.0 / (1.0 + jnp.exp(-za))
    for q in range(lw_ref.shape[0]):
        lw_ref[q] = lw[:, q * SLAB:(q + 1) * SLAB]
        a_ref[q] = a[:, q * SLAB:(q + 1) * SLAB].astype(BF16)


def _rwkv_mix(x2d, mu, w0, w1, w2, a0, a1, a2, *, seq_len):
    bt, c = x2d.shape
    tm = min(256, seq_len)
    nq = c // SLAB
    kern = functools.partial(_rwkv_mix_kernel, seq_tiles=seq_len // tm)
    full = lambda shape: pl.BlockSpec(shape, lambda i: (0,) * len(shape))
    return pl.pallas_call(
        kern,
        grid=(bt // tm,),
        in_specs=[
            pl.BlockSpec((tm, c), lambda i: (i, 0)),
            pl.BlockSpec((8, c), lambda i: (jnp.maximum(i * (tm // 8) - 1, 0), 0)),
            full((6, c)),
            full((1, c)), full((c, LORA_PAD)), full((LORA_PAD, c)),
            full((1, c)), full((c, LORA_PAD)), full((LORA_PAD, c)),
        ],
        out_specs=[
            pl.BlockSpec((4, tm, c), lambda i: (0, i, 0)),
            pl.BlockSpec((nq, tm, SLAB), lambda i: (0, i, 0)),
            pl.BlockSpec((nq, tm, SLAB), lambda i: (0, i, 0)),
        ],
        out_shape=[
            jax.ShapeDtypeStruct((4, bt, c), BF16),
            jax.ShapeDtypeStruct((nq, bt, SLAB), F32),
            jax.ShapeDtypeStruct((nq, bt, SLAB), BF16),
        ],
        compiler_params=pltpu.CompilerParams(
            dimension_semantics=("arbitrary",),
            vmem_limit_bytes=VMEM_LIMIT),
    )(x2d, x2d, mu, w0, w1, w2, a0, a1, a2)


def _stream_matmul_kernel(x_ref, w_ref, o_ref):
    acc = _mm(x_ref[0], w_ref[0])
    for q in range(o_ref.shape[1]):
        o_ref[0, q] = acc[:, q * SLAB:(q + 1) * SLAB].astype(BF16)


def _stream_matmul(xs, w_in):
    ns, bt, c = xs.shape
    tm = min(1024, bt)
    tn = 1024
    return pl.pallas_call(
        _stream_matmul_kernel,
        grid=(bt // tm, ns, c // tn),
        in_specs=[
            pl.BlockSpec((1, tm, c), lambda i, s, n: (s, i, 0)),
            pl.BlockSpec((1, c, tn), lambda i, s, n: (s, 0, n)),
        ],
        out_specs=pl.BlockSpec((1, tn // SLAB, tm, SLAB), lambda i, s, n: (s, n, i, 0)),
        out_shape=jax.ShapeDtypeStruct((ns, c // SLAB, bt, SLAB), BF16),
        compiler_params=pltpu.CompilerParams(
            dimension_semantics=("arbitrary", "arbitrary", "arbitrary"),
            vmem_limit_bytes=VMEM_LIMIT),
    )(xs, w_in)


def _wkv_masks():
    ti = lax.broadcasted_iota(jnp.int32, (PAIR, PAIR), 0)
    si = lax.broadcasted_iota(jnp.int32, (PAIR, PAIR), 1)
    same = (ti // WKV_CHUNK) == (si // WKV_CHUNK)
    tl = ti % WKV_CHUNK
    sl = si % WKV_CHUNK
    levels = []
    b = 1
    while b < WKV_CHUNK:
        levels.append(same & ((tl // b) == (sl // b) + 1) & (((tl // b) % 2) == 1))
        b *= 2
    return dict(same=same, eye=ti == si,
                strict=same & (sl < tl), incl=same & (sl <= tl),
                anti_strict=(~same) & (sl < tl), anti_incl=(~same) & (sl <= tl),
                levels=levels)


def _wkv_kernel(r_ref, k_ref, v_ref, g_ref, lw_ref, a_ref, kk_ref, ka_ref, rk_ref, gng_ref, gnb_ref,
                o_ref, h_ref):
    seq = o_ref.shape[1]
    ch = WKV_CHUNK
    gl = WKV_GROUP * ch
    masks = _wkv_masks()
    onesbd = jnp.where(masks["same"], 1.0, 0.0).astype(BF16)
    lane = lax.broadcasted_iota(jnp.int32, (ch, PAIR), 1)
    m0 = lane < RWKV_HEAD
    gt = lax.broadcasted_iota(jnp.int32, (gl, gl), 0)
    gs = lax.broadcasted_iota(jnp.int32, (gl, gl), 1)
    tri = jnp.where(((gt // ch) == (gs // ch)) & (gs <= gt), 1.0, 0.0).astype(BF16)

    def head_sum(x):
        return _mm(x.astype(BF16), onesbd)

    @pl.when(pl.program_id(2) == 0)
    def _():
        h_ref[...] = jnp.zeros(h_ref.shape, F32)

    pps = SLAB // PAIR
    npair = o_ref.shape[0] * pps
    inv_n = 1.0 / RWKV_HEAD

    def group(gi, carry):
        row0 = pl.multiple_of(gi * gl, gl)
        rows = pl.ds(row0, gl)
        pre = []
        for p in range(npair):
            sb = p // pps
            lanes = slice((p % pps) * PAIR, (p % pps + 1) * PAIR)
            r = r_ref[0, sb, rows, lanes].astype(F32)
            k = k_ref[0, sb, rows, lanes].astype(F32)
            v = v_ref[0, sb, rows, lanes].astype(F32)
            lw = lw_ref[sb, rows, lanes]
            a = a_ref[sb, rows, lanes].astype(F32)
            l1, l2 = _split2(lw)
            c = _mm(tri, l1) + _mm(tri, l2)
            pw = jnp.exp(c)
            pinv = jnp.exp(-c)
            pex = jnp.exp(c - lw)
            kkr = k * kk_ref[sb, :, lanes]
            kk = kkr * lax.rsqrt(jnp.maximum(head_sum(kkr * kkr), 1e-24))
            kmod = k * (1.0 + (a - 1.0) * ka_ref[sb, :, lanes])
            bonus = head_sum(r * kmod * rk_ref[sb, :, lanes]) * v
            pre.append(dict(v=v, pw=pw, rt=r * pw, bt=kk * pex, kt=kmod * pinv, at=a * kk * pinv,
                            bonus=bonus))

        units = []
        for p in range(npair):
            d = pre[p]
            for j in range(WKV_GROUP):
                cs = slice(j * ch, (j + 1) * ch)
                units.append(dict(rt=d["rt"][cs], bt=d["bt"][cs], kt=d["kt"][cs], at=d["at"][cs],
                                  v=d["v"][cs], pl_row=d["pw"][j * ch + ch - 1:j * ch + ch, :]))

        z = jnp.zeros((ch, PAIR), F32)
        for u in units:
            rt_j, bt_j, kt_j, at_j = u["rt"], u["bt"], u["kt"], u["at"]
            lhs0 = jnp.concatenate([jnp.where(m0, rt_j, z), jnp.where(m0, bt_j, z)], 0).astype(BF16)
            lhs1 = jnp.concatenate([jnp.where(m0, z, rt_j), jnp.where(m0, z, bt_j)], 0).astype(BF16)
            rhs0 = jnp.concatenate([at_j, kt_j], 0).astype(BF16)
            rhs1 = jnp.concatenate([kt_j, at_j], 0).astype(BF16)
            u["a0"] = _mm_nt(lhs0, rhs0)
            u["a1"] = _mm_nt(lhs1, rhs1)
        for u in units:
            a0, a1 = u.pop("a0"), u.pop("a1")
            rblk = jnp.concatenate([a0[:ch], a1[:ch]], 0)
            bblk = jnp.concatenate([a0[ch:], a1[ch:]], 0)
            u["nmat"] = jnp.where(masks["strict"], bblk, 0.0)
            u["abk_anti"] = jnp.where(masks["anti_strict"], bblk, 0.0).astype(BF16)
            u["ara_bd"] = jnp.where(masks["incl"], rblk, 0.0).astype(BF16)
            u["ark_anti"] = jnp.where(masks["anti_incl"], rblk, 0.0).astype(BF16)
            v_j, bt_j = u["v"], u["bt"]
            u["v_sw"] = jnp.concatenate([jnp.where(m0, z, v_j), jnp.where(m0, v_j, z)], 0).astype(BF16)
            u["b_st"] = jnp.concatenate([jnp.where(m0, bt_j, z), jnp.where(m0, z, bt_j)], 0).astype(BF16)
        for u in units:
            u["w2"] = _mm(u["abk_anti"], u["v_sw"]).astype(BF16)

        eye = jnp.where(masks["eye"], 1.0, 0.0).astype(F32)
        for u in units:
            u["t"] = eye - jnp.where(masks["levels"][0], u["nmat"], 0.0)
        for li, lvl in enumerate(masks["levels"][1:]):
            blk = 2 << li
            if blk % 8 != 0:
                for u in units:
                    u["tb"] = u["t"].astype(BF16)
                    u["x"] = _mm(u["tb"], jnp.where(lvl, u["nmat"], 0.0).astype(BF16)).astype(BF16)
                for u in units:
                    u["t"] = u["t"] - _mm(u.pop("x"), u.pop("tb"))
            else:
                nblk = PAIR // blk
                for u in units:
                    u["tb"] = u["t"].astype(BF16)
                    t_odd = jnp.concatenate([u["t"][i * blk:(i + 1) * blk] for i in range(1, nblk, 2)], 0)
                    u["x"] = _mm(t_odd.astype(BF16), jnp.where(lvl, u["nmat"], 0.0).astype(BF16)).astype(BF16)
                for u in units:
                    y = _mm(u.pop("x"), u.pop("tb"))
                    t = u["t"]
                    u["t"] = jnp.concatenate(
                        [t[i * blk:(i + 1) * blk] - y[(i // 2) * blk:(i // 2 + 1) * blk] if i % 2
                         else t[i * blk:(i + 1) * blk] for i in range(nblk)], 0)

        for u in units:
            u["tbv"] = _mm(u["t"].astype(BF16), jnp.concatenate([u["b_st"], u["w2"]], 1))
        zero_st = jnp.zeros((PAIR, PAIR), BF16)
        for u in units:
            tbv = u["tbv"]
            bp_st = tbv[:, :PAIR]
            vp_st = tbv[:, PAIR:]
            rhs_y = jnp.concatenate([
                jnp.concatenate([u["v_sw"], zero_st], 1),
                jnp.concatenate([(-vp_st).astype(BF16), (-bp_st).astype(BF16)], 1)], 0)
            u["yr"] = _mm(jnp.concatenate([u["ark_anti"], u["ara_bd"]], 1), rhs_y)
            bp = bp_st[:ch] + bp_st[ch:]
            vp = vp_st[:ch] + vp_st[ch:]
            khat = u["kt"] * u["pl_row"]
            ahat = u["at"] * u["pl_row"]
            zc = jnp.zeros((ch, PAIR), BF16)
            lhs_t = jnp.concatenate([u["v"], -vp, -ahat], 0).astype(BF16)
            rhs_t = jnp.concatenate([
                jnp.concatenate([khat.astype(BF16), zc], 1),
                jnp.concatenate([ahat.astype(BF16), zc], 1),
                jnp.concatenate([zc, bp.astype(BF16)], 1)], 0)
            u["gm"] = _mm_tn(lhs_t, rhs_t)
        for u in units:
            yr, gm = u["yr"], u["gm"]
            u["y0"] = yr[:ch, :PAIR] + yr[ch:, :PAIR]
            u["rp"] = (u["rt"] + yr[:ch, PAIR:] + yr[ch:, PAIR:]).astype(BF16)
            u["g_t"] = jnp.where(masks["same"], gm[:, :PAIR], 0.0)
            u["m2"] = jnp.where(masks["same"], gm[:, PAIR:], 0.0).astype(BF16)

        hts = [h_ref[p] for p in range(npair)]
        ys = [[] for _ in range(npair)]
        for j in range(WKV_GROUP):
            for p in range(npair):
                u = units[p * WKV_GROUP + j]
                htb = hts[p].astype(BF16)
                ys[p].append(_mm_nt(u["rp"], htb) + u["y0"])
                hts[p] = hts[p] * u["pl_row"] + _mm_nt(htb, u["m2"]) + u["g_t"]
        for p in range(npair):
            h_ref[p] = hts[p]

        for p in range(npair):
            sb = p // pps
            lanes = slice((p % pps) * PAIR, (p % pps + 1) * PAIR)
            y = jnp.concatenate(ys[p], 0)
            mean = head_sum(y) * inv_n
            yc = y - mean
            var = head_sum(yc * yc) * inv_n
            yn = yc * lax.rsqrt(var + GN_EPS) * gng_ref[sb, :, lanes] + gnb_ref[sb, :, lanes]
            g = g_ref[0, sb, rows, lanes].astype(F32)
            out = (yn + pre[p]["bonus"]) * (g / (1.0 + jnp.exp(-g)))
            o_ref[sb, rows, lanes] = out.astype(BF16)
        return carry

    lax.fori_loop(0, seq // gl, group, 0)


def _wkv(proj, lw, a, kk, ka, rk, gng, gnb, *, batch, seq_len):
    nq = lw.shape[0]
    bt = lw.shape[1]
    ns = WKV_SLABS
    tb = min(WKV_TIME_BLOCK, seq_len)
    nt = seq_len // tb
    pspec = lambda s: pl.BlockSpec((1, ns, tb, SLAB), lambda b, q, t, s=s: (s, q, b * nt + t, 0))
    aspec = pl.BlockSpec((ns, tb, SLAB), lambda b, q, t: (q, b * nt + t, 0))
    vspec = pl.BlockSpec((ns, 1, SLAB), lambda b, q, t: (q, 0, 0))
    return pl.pallas_call(
        _wkv_kernel,
        grid=(batch, nq // ns, nt),
        in_specs=[pspec(0), pspec(1), pspec(2), pspec(3), aspec, aspec, vspec, vspec, vspec, vspec, vspec],
        out_specs=aspec,
        out_shape=jax.ShapeDtypeStruct((nq, bt, SLAB), BF16),
        scratch_shapes=[pltpu.VMEM((ns * SLAB // PAIR, PAIR, PAIR), F32)],
        compiler_params=pltpu.CompilerParams(
            dimension_semantics=("arbitrary", "arbitrary", "arbitrary"),
            vmem_limit_bytes=VMEM_LIMIT),
    )(proj, proj, proj, proj, lw, a, kk, ka, rk, gng, gnb)


def _out_ln_kernel(y_ref, x_ref, w_ref, g_ref, b_ref, *o_refs):
    acc = _mm(y_ref[0], w_ref[0])
    for q in range(1, y_ref.shape[0]):
        acc = acc + _mm(y_ref[q], w_ref[q])
    h = DEEPNORM_ALPHA * x_ref[...] + acc
    mu = jnp.mean(h, axis=-1, keepdims=True)
    hc = h - mu
    var = jnp.mean(hc * hc, axis=-1, keepdims=True)
    out = hc * lax.rsqrt(var + LN_EPS) * g_ref[...] + b_ref[...]
    o_refs[0][...] = out
    if len(o_refs) > 1:
        o_refs[1][...] = out.astype(BF16)


def _out_ln(y_slab, x2d, w_slab, ln_g, ln_b, *, with_bf16):
    nq, bt, _ = y_slab.shape
    c = x2d.shape[1]
    tm = min(256, bt)
    row = pl.BlockSpec((tm, c), lambda i: (i, 0))
    out_specs = [row]
    out_shape = [jax.ShapeDtypeStruct((bt, c), F32)]
    if with_bf16:
        out_specs.append(row)
        out_shape.append(jax.ShapeDtypeStruct((bt, c), BF16))
    return pl.pallas_call(
        _out_ln_kernel,
        grid=(bt // tm,),
        in_specs=[
            pl.BlockSpec((nq, tm, SLAB), lambda i: (0, i, 0)),
            row,
            pl.BlockSpec((nq, SLAB, c), lambda i: (0, 0, 0)),
            pl.BlockSpec((1, c), lambda i: (0, 0)),
            pl.BlockSpec((1, c), lambda i: (0, 0)),
        ],
        out_specs=out_specs,
        out_shape=out_shape,
        compiler_params=pltpu.CompilerParams(
            dimension_semantics=("arbitrary",),
            vmem_limit_bytes=VMEM_LIMIT),
    )(y_slab, x2d, w_slab, ln_g, ln_b)


def _slab_matmul_kernel(x_ref, w_ref, o_ref, *, q_tiles):
    acc = _mm(x_ref[...], w_ref[...])
    acc = acc * jnp.where(pl.program_id(1) < q_tiles, DIFF_HEAD_DIM ** -0.5, 1.0)
    for q in range(o_ref.shape[0]):
        o_ref[q] = acc[:, q * SLAB:(q + 1) * SLAB].astype(BF16)


def _slab_matmul(x_bf, w_bf, *, q_cols):
    bt, c = x_bf.shape
    n_out = w_bf.shape[1]
    tm = min(1024, bt)
    tn = 1024
    return pl.pallas_call(
        functools.partial(_slab_matmul_kernel, q_tiles=q_cols // tn),
        grid=(bt // tm, n_out // tn),
        in_specs=[
            pl.BlockSpec((tm, c), lambda i, n: (i, 0)),
            pl.BlockSpec((c, tn), lambda i, n: (0, n)),
        ],
        out_specs=pl.BlockSpec((tn // SLAB, tm, SLAB), lambda i, n: (n, i, 0)),
        out_shape=jax.ShapeDtypeStruct((n_out // SLAB, bt, SLAB), BF16),
        compiler_params=pltpu.CompilerParams(
            dimension_semantics=("arbitrary", "arbitrary"),
            vmem_limit_bytes=VMEM_LIMIT),
    )(x_bf, w_bf)


def _attn_kernel(q_ref, k_ref, v_ref, g_ref, lam_ref, sg_ref, o_ref, m_ref, l_ref, acc_ref,
                 *, tq, lam_init):
    nh = q_ref.shape[0]
    hb = pl.program_id(1)
    qi = pl.program_id(2)
    dh = DIFF_HEAD_DIM
    slopes = [jnp.exp2(-(hb * nh + hh + 1).astype(F32)) for hh in range(nh)]
    lam_p = lam_ref[...]
    lam = (jnp.exp(jnp.sum(lam_p[0:1] * lam_p[1:2])) - jnp.exp(jnp.sum(lam_p[2:3] * lam_p[3:4]))
           + lam_init)

    ii = lax.broadcasted_iota(jnp.int32, (tq, tq), 0)
    jj = lax.broadcasted_iota(jnp.int32, (tq, tq), 1)
    dmat = (ii - jj).astype(F32)
    allowed = (jj // MASK_CHUNK) <= (ii // MASK_CHUNK)

    m_ref[...] = jnp.full(m_ref.shape, -jnp.inf, F32)
    l_ref[...] = jnp.zeros(l_ref.shape, F32)
    acc_ref[...] = jnp.zeros(acc_ref.shape, F32)

    nl = tq // 128
    streams = [(hh, u) for hh in range(nh) for u in range(2)]

    def step(blocks):
        nb = len(blocks)
        kbs = [[None] * nb for _ in range(nh)]
        vbs = [[None] * nb for _ in range(nh)]
        biases = [[None] * nb for _ in range(nh)]
        for b, (j, masked) in enumerate(blocks):
            k0 = pl.multiple_of(j * tq, tq)
            base = (k0 + ii).astype(F32) - jnp.abs(dmat) if masked else (k0 + jj[0:1, :]).astype(F32)
            for hh in range(nh):
                biases[hh][b] = slopes[hh] * base
                kbs[hh][b] = k_ref[hh, pl.ds(k0, tq), :]
                vbs[hh][b] = v_ref[hh, pl.ds(k0, tq), :]
        s = [[None] * nb for _ in streams]
        for w, (hh, u) in enumerate(streams):
            lanes = slice(u * dh, (u + 1) * dh)
            for b, (j, masked) in enumerate(blocks):
                su = _mm_nt(q_ref[hh, :, lanes], kbs[hh][b][:, lanes]) + biases[hh][b]
                if masked:
                    su = jnp.where(allowed, su, -jnp.inf)
                s[w][b] = su
        m_new, p = [], [[None] * nb for _ in streams]
        for w in range(len(streams)):
            pieces = [s[w][b][:, c * 128:(c + 1) * 128] for b in range(nb) for c in range(nl)]
            smax = functools.reduce(jnp.maximum, pieces)
            m_new.append(jnp.maximum(m_ref[w], jnp.max(smax, axis=1, keepdims=True)))
            mrep = jnp.concatenate([m_new[w]] * nl, axis=1)
            for b in range(nb):
                p[w][b] = jnp.exp(s[w][b] - mrep)
        pv = [sum(_mm(p[w][b].astype(BF16), vbs[hh][b]) for b in range(nb))
              for w, (hh, u) in enumerate(streams)]
        for w in range(len(streams)):
            pieces = [p[w][b][:, c * 128:(c + 1) * 128] for b in range(nb) for c in range(nl)]
            psum = jnp.sum(functools.reduce(jnp.add, pieces), axis=1, keepdims=True)
            corr = jnp.exp(m_ref[w] - m_new[w])
            l_ref[w] = corr * l_ref[w] + psum
            acc_ref[w] = jnp.concatenate([corr] * (SLAB // 128), axis=1) * acc_ref[w] + pv[w]
            m_ref[w] = m_new[w]

    def body(jp, carry):
        step([(2 * jp, False), (2 * jp + 1, False)])
        return carry

    lax.fori_loop(0, qi // 2, body, 0)

    @pl.when(qi % 2 == 1)
    def _():
        step([(qi - 1, False), (qi, True)])

    @pl.when(qi % 2 == 0)
    def _():
        step([(qi, True)])

    for hh in range(nh):
        inv_l = [1.0 / l_ref[2 * hh + u][:, 0:1] for u in range(2)]
        o = acc_ref[2 * hh] * inv_l[0] - lam * (acc_ref[2 * hh + 1] * inv_l[1])
        o = (o * lax.rsqrt(jnp.mean(o * o, axis=-1, keepdims=True) + SUBLN_EPS) * sg_ref[...]
             * (1.0 - lam_init))
        g = g_ref[hh].astype(F32)
        o_ref[hh] = (o * (g / (1.0 + jnp.exp(-g)))).astype(BF16)


def _diff_attention(qgkv, lam_p, subln_g, *, batch, seq_len, layer):
    nq = qgkv.shape[0] // 4
    bt = qgkv.shape[1]
    tq = min(256, seq_len)
    nh = ATTN_HEADS
    nhb = nq // nh
    nqb = seq_len // tq
    lam_init = 0.8 - 0.6 * math.exp(-0.3 * layer)
    kern = functools.partial(_attn_kernel, tq=tq, lam_init=lam_init)
    return pl.pallas_call(
        kern,
        grid=(batch, nhb, nqb),
        in_specs=[
            pl.BlockSpec((nh, tq, SLAB), lambda b, h, i: (h, b * nqb + i, 0)),
            pl.BlockSpec((nh, seq_len, SLAB), lambda b, h, i: (2 * nhb + h, b, 0)),
            pl.BlockSpec((nh, seq_len, SLAB), lambda b, h, i: (3 * nhb + h, b, 0)),
            pl.BlockSpec((nh, tq, SLAB), lambda b, h, i: (nhb + h, b * nqb + i, 0)),
            pl.BlockSpec(lam_p.shape, lambda b, h, i: (0, 0)),
            pl.BlockSpec((1, SLAB), lambda b, h, i: (0, 0)),
        ],
        out_specs=pl.BlockSpec((nh, tq, SLAB), lambda b, h, i: (h, b * nqb + i, 0)),
        out_shape=jax.ShapeDtypeStruct((nq, bt, SLAB), BF16),
        scratch_shapes=[pltpu.VMEM((2 * nh, tq, 128), F32), pltpu.VMEM((2 * nh, tq, 128), F32),
                        pltpu.VMEM((2 * nh, tq, SLAB), F32)],
        compiler_params=pltpu.CompilerParams(
            dimension_semantics=("arbitrary", "arbitrary", "arbitrary"),
            vmem_limit_bytes=VMEM_LIMIT),
    )(qgkv, qgkv, qgkv, qgkv, lam_p, subln_g)


def _pad_cols(w):
    return jnp.pad(w, ((0, 0), (0, LORA_PAD - w.shape[1])))


def _pad_rows(w):
    return jnp.pad(w, ((0, LORA_PAD - w.shape[0]), (0, 0)))


def kernel(x, a_mu_proj, a_mu_lora, a_w_in, a_w0, a_w1, a_w2, a_a0, a_a1, a_a2, a_k_k, a_k_a, a_r_k,
           a_gn_g, a_gn_b, a_w_out, w_k_shared, w_v_shared, b_w_qg, b_lambda, b_subln_g, b_w_out,
           ln_g, ln_b):
    batch, seq_len, c = x.shape
    assert a_w_in.shape[0] == 1 and b_w_qg.shape[0] == 1 and ln_g.shape[0] == DEPTH
    assert c % SLAB == 0 and seq_len % (WKV_CHUNK * WKV_GROUP) == 0
    bt = batch * seq_len
    nq = c // SLAB
    x2d = x.reshape(bt, c)
    slab_vec = lambda p: p.reshape(nq, 1, SLAB)
    slab_rows = lambda w: w.astype(BF16).reshape(nq, SLAB, w.shape[1])

    mu = jnp.concatenate([a_mu_proj[0], a_mu_lora[0]], axis=0)
    xs, lw, a = _rwkv_mix(
        x2d, mu,
        a_w0[0][None], _pad_cols(a_w1[0]).astype(BF16), _pad_rows(a_w2[0]).astype(BF16),
        a_a0[0][None], _pad_cols(a_a1[0]).astype(BF16), _pad_rows(a_a2[0]).astype(BF16),
        seq_len=seq_len)
    proj = _stream_matmul(xs, a_w_in[0].astype(BF16))
    yg = _wkv(proj, lw, a, slab_vec(a_k_k[0]), slab_vec(a_k_a[0]), slab_vec(a_r_k[0]),
              slab_vec(a_gn_g[0]), slab_vec(a_gn_b[0]), batch=batch, seq_len=seq_len)
    x1, x1_bf = _out_ln(yg, x2d, slab_rows(a_w_out[0]), ln_g[0][None], ln_b[0][None], with_bf16=True)

    w_all = jnp.concatenate([b_w_qg[0], w_k_shared, w_v_shared], axis=1).astype(BF16)
    qgkv = _slab_matmul(x1_bf, w_all, q_cols=c)
    og = _diff_attention(qgkv, b_lambda[0], b_subln_g[0][None], batch=batch, seq_len=seq_len, layer=1)
    (out,) = _out_ln(og, x1, slab_rows(b_w_out[0]), ln_g[1][None], ln_b[1][None], with_bf16=False)
    return out.reshape(batch, seq_len, c)
```

```python
import functools
import math

import jax
import jax.numpy as jnp
from jax import lax
from jax.experimental import pallas as pl
from jax.experimental.pallas import tpu as pltpu

F32 = jnp.float32
BF16 = jnp.bfloat16

DEPTH = 2
RWKV_HEAD = 64
MASK_CHUNK = 64
DIFF_HEAD_DIM = 128
GN_EPS = 64e-5
SUBLN_EPS = 1e-5
LN_EPS = 1e-5
DEEPNORM_ALPHA = (2.0 * DEPTH) ** 0.25

SLAB = 256
PAIR = 128
LORA_PAD = 128
WKV_CHUNK = 64
WKV_GROUP = 4
WKV_SLABS = 2
WKV_TIME_BLOCK = 2048
ATTN_HEADS = 4
VMEM_LIMIT = 56 * 1024 * 1024


def _mm(a, b):
    return jnp.dot(a, b, preferred_element_type=F32)


def _mm_nt(a, b):
    return lax.dot_general(a, b, (((1,), (1,)), ((), ())), preferred_element_type=F32)


def _mm_tn(a, b):
    return lax.dot_general(a, b, (((0,), (0,)), ((), ())), preferred_element_type=F32)


def _split2(x):
    hi = x.astype(BF16)
    lo = (x - hi.astype(F32)).astype(BF16)
    return hi, lo


def _rwkv_mix_kernel(x_ref, xp_ref, mu_ref, w0_ref, w1_ref, w2_ref, a0_ref, a1_ref, a2_ref,
                     xs_ref, lw_ref, a_ref, *, seq_tiles):
    i = pl.program_id(0)
    x = x_ref[...]
    prev_last = xp_ref[7:8, :]
    prev_last = jnp.where((i % seq_tiles) == 0, 0.0, prev_last)
    row = lax.broadcasted_iota(jnp.int32, x.shape, 0)
    xsh = jnp.where(row == 0, prev_last, pltpu.roll(x, 1, axis=0))
    xx = xsh - x
    for st in range(4):
        xs_ref[st] = (x + xx * mu_ref[st:st + 1, :]).astype(BF16)
    xw = (x + xx * mu_ref[4:5, :]).astype(BF16)
    xa = (x + xx * mu_ref[5:6, :]).astype(BF16)
    hw = jnp.tanh(_mm(xw, w1_ref[...]))
    z = w0_ref[...] + _mm(hw.astype(BF16), w2_ref[...])
    sp = jnp.maximum(-z, 0.0) + jnp.log1p(jnp.exp(-jnp.abs(z)))
    lw = -jnp.exp(-sp - 0.5)
    ha = _mm(xa, a1_ref[...])
    za = a0_ref[...] + _mm(ha.astype(BF16), a2_ref[...])
    a = 1.0 / (1.0 + jnp.exp(-za))
    for q in range(lw_ref.shape[0]):
        lw_ref[q] = lw[:, q * SLAB:(q + 1) * SLAB]
        a_ref[q] = a[:, q * SLAB:(q + 1) * SLAB].astype(BF16)


def _rwkv_mix(x2d, mu, w0, w1, w2, a0, a1, a2, *, seq_len):
    bt, c = x2d.shape
    tm = min(256, seq_len)
    nq = c // SLAB
    kern = functools.partial(_rwkv_mix_kernel, seq_tiles=seq_len // tm)
    full = lambda shape: pl.BlockSpec(shape, lambda i: (0,) * len(shape))
    return pl.pallas_call(
        kern,
        grid=(bt // tm,),
        in_specs=[
            pl.BlockSpec((tm, c), lambda i: (i, 0)),
            pl.BlockSpec((8, c), lambda i: (jnp.maximum(i * (tm // 8) - 1, 0), 0)),
            full((6, c)),
            full((1, c)), full((c, LORA_PAD)), full((LORA_PAD, c)),
            full((1, c)), full((c, LORA_PAD)), full((LORA_PAD, c)),
        ],
        out_specs=[
            pl.BlockSpec((4, tm, c), lambda i: (0, i, 0)),
            pl.BlockSpec((nq, tm, SLAB), lambda i: (0, i, 0)),
            pl.BlockSpec((nq, tm, SLAB), lambda i: (0, i, 0)),
        ],
        out_shape=[
            jax.ShapeDtypeStruct((4, bt, c), BF16),
            jax.ShapeDtypeStruct((nq, bt, SLAB), F32),
            jax.ShapeDtypeStruct((nq, bt, SLAB), BF16),
        ],
        compiler_params=pltpu.CompilerParams(
            dimension_semantics=("arbitrary",),
            vmem_limit_bytes=VMEM_LIMIT),
    )(x2d, x2d, mu, w0, w1, w2, a0, a1, a2)


def _stream_matmul_kernel(x_ref, w_ref, o_ref):
    acc = _mm(x_ref[0], w_ref[0])
    for q in range(o_ref.shape[1]):
        o_ref[0, q] = acc[:, q * SLAB:(q + 1) * SLAB].astype(BF16)


def _stream_matmul(xs, w_in):
    ns, bt, c = xs.shape
    tm = min(1024, bt)
    tn = 1024
    return pl.pallas_call(
        _stream_matmul_kernel,
        grid=(bt // tm, ns, c // tn),
        in_specs=[
            pl.BlockSpec((1, tm, c), lambda i, s, n: (s, i, 0)),
            pl.BlockSpec((1, c, tn), lambda i, s, n: (s, 0, n)),
        ],
        out_specs=pl.BlockSpec((1, tn // SLAB, tm, SLAB), lambda i, s, n: (s, n, i, 0)),
        out_shape=jax.ShapeDtypeStruct((ns, c // SLAB, bt, SLAB), BF16),
        compiler_params=pltpu.CompilerParams(
            dimension_semantics=("arbitrary", "arbitrary", "arbitrary"),
            vmem_limit_bytes=VMEM_LIMIT),
    )(xs, w_in)


def _wkv_masks():
    ti = lax.broadcasted_iota(jnp.int32, (PAIR, PAIR), 0)
    si = lax.broadcasted_iota(jnp.int32, (PAIR, PAIR), 1)
    same = (ti // WKV_CHUNK) == (si // WKV_CHUNK)
    tl = ti % WKV_CHUNK
    sl = si % WKV_CHUNK
    levels = []
    b = 1
    while b < WKV_CHUNK:
        levels.append(same & ((tl // b) == (sl // b) + 1) & (((tl // b) % 2) == 1))
        b *= 2
    return dict(same=same, eye=ti == si,
                strict=same & (sl < tl), incl=same & (sl <= tl),
                anti_strict=(~same) & (sl < tl), anti_incl=(~same) & (sl <= tl),
                levels=levels)


def _wkv_kernel(r_ref, k_ref, v_ref, g_ref, lw_ref, a_ref, kk_ref, ka_ref, rk_ref, gng_ref, gnb_ref,
                o_ref, h_ref, rp_s, y0_s, m2_s, gt_s, plr_s, bonus_s):
    seq = o_ref.shape[1]
    ch = WKV_CHUNK
    gl = WKV_GROUP * ch
    masks = _wkv_masks()
    onesbd = jnp.where(masks["same"], 1.0, 0.0).astype(BF16)
    lane = lax.broadcasted_iota(jnp.int32, (ch, PAIR), 1)
    m0 = lane < RWKV_HEAD
    gt = lax.broadcasted_iota(jnp.int32, (gl, gl), 0)
    gs = lax.broadcasted_iota(jnp.int32, (gl, gl), 1)
    tri = jnp.where(((gt // ch) == (gs // ch)) & (gs <= gt), 1.0, 0.0).astype(BF16)

    glane = lax.broadcasted_iota(jnp.int32, (gl, PAIR), 1)
    g0 = glane < RWKV_HEAD

    def head_sum(x):
        s0 = jnp.sum(jnp.where(g0, x, 0.0), axis=1, keepdims=True)
        s1 = jnp.sum(jnp.where(g0, 0.0, x), axis=1, keepdims=True)
        return jnp.where(g0, s0, s1)

    @pl.when(pl.program_id(2) == 0)
    def _():
        h_ref[...] = jnp.zeros(h_ref.shape, F32)

    pps = SLAB // PAIR
    npair = o_ref.shape[0] * pps
    inv_n = 1.0 / RWKV_HEAD

    nunit = npair * WKV_GROUP

    def rows_of(gi):
        return pl.ds(pl.multiple_of(gi * gl, gl), gl)

    def state_pass(gi):
        slot = gi % 2
        rows = rows_of(gi)
        st = dict(hts=None, ys=[[] for _ in range(npair)])

        def chunk_step(j):
            if j == 0:
                st["hts"] = [h_ref[p] for p in range(npair)]
            for p in range(npair):
                ui = p * WKV_GROUP + j
                htb = st["hts"][p].astype(BF16)
                st["ys"][p].append(_mm_nt(rp_s[slot, ui], htb) + y0_s[slot, ui])
                st["hts"][p] = st["hts"][p] * plr_s[slot, ui] + _mm_nt(htb, m2_s[slot, ui]) + gt_s[slot, ui]
            if j == WKV_GROUP - 1:
                for p in range(npair):
                    h_ref[p] = st["hts"][p]

        def finish():
            for p in range(npair):
                sb = p // pps
                lanes = slice((p % pps) * PAIR, (p % pps + 1) * PAIR)
                y = jnp.concatenate(st["ys"][p], 0)
                mean = head_sum(y) * inv_n
                yc = y - mean
                var = head_sum(yc * yc) * inv_n
                yn = yc * lax.rsqrt(var + GN_EPS) * gng_ref[sb, :, lanes] + gnb_ref[sb, :, lanes]
                g = g_ref[0, sb, rows, lanes].astype(F32)
                out = (yn + bonus_s[slot, p]) * (g / (1.0 + jnp.exp(-g)))
                o_ref[sb, rows, lanes] = out.astype(BF16)

        return [functools.partial(chunk_step, j) for j in range(WKV_GROUP)] + [finish]

    def matmul_pass(gi, hooks):
        slot = gi % 2
        rows = rows_of(gi)
        hooks = list(hooks)

        def run_hook():
            if hooks:
                hooks.pop(0)()

        pre = []
        for p in range(npair):
            sb = p // pps
            lanes = slice((p % pps) * PAIR, (p % pps + 1) * PAIR)
            r = r_ref[0, sb, rows, lanes].astype(F32)
            k = k_ref[0, sb, rows, lanes].astype(F32)
            v = v_ref[0, sb, rows, lanes].astype(F32)
            lw = lw_ref[sb, rows, lanes]
            a = a_ref[sb, rows, lanes].astype(F32)
            l1, l2 = _split2(lw)
            c = _mm(tri, l1) + _mm(tri, l2)
            pw = jnp.exp(c)
            pinv = jnp.exp(-c)
            pex = jnp.exp(c - lw)
            kkr = k * kk_ref[sb, :, lanes]
            kk = kkr * lax.rsqrt(jnp.maximum(head_sum(kkr * kkr), 1e-24))
            kmod = k * (1.0 + (a - 1.0) * ka_ref[sb, :, lanes])
            bonus = head_sum(r * kmod * rk_ref[sb, :, lanes]) * v
            bonus_s[slot, p] = bonus
            pre.append(dict(v=v, pw=pw, rt=r * pw, bt=kk * pex, kt=kmod * pinv, at=a * kk * pinv))

        units = []
        for p in range(npair):
            d = pre[p]
            for j in range(WKV_GROUP):
                cs = slice(j * ch, (j + 1) * ch)
                units.append(dict(rt=d["rt"][cs], bt=d["bt"][cs], kt=d["kt"][cs], at=d["at"][cs],
                                  v=d["v"][cs], pl_row=d["pw"][j * ch + ch - 1:j * ch + ch, :]))

        z = jnp.zeros((ch, PAIR), F32)
        for u in units:
            rt_j, bt_j, kt_j, at_j = u["rt"], u["bt"], u["kt"], u["at"]
            lhs0 = jnp.concatenate([jnp.where(m0, rt_j, z), jnp.where(m0, bt_j, z)], 0).astype(BF16)
            lhs1 = jnp.concatenate([jnp.where(m0, z, rt_j), jnp.where(m0, z, bt_j)], 0).astype(BF16)
            rhs0 = jnp.concatenate([at_j, kt_j], 0).astype(BF16)
            rhs1 = jnp.concatenate([kt_j, at_j], 0).astype(BF16)
            u["a0"] = _mm_nt(lhs0, rhs0)
            u["a1"] = _mm_nt(lhs1, rhs1)
        for u in units:
            a0, a1 = u.pop("a0"), u.pop("a1")
            rblk = jnp.concatenate([a0[:ch], a1[:ch]], 0)
            bblk = jnp.concatenate([a0[ch:], a1[ch:]], 0)
            u["nmat"] = jnp.where(masks["strict"], bblk, 0.0)
            u["abk_anti"] = jnp.where(masks["anti_strict"], bblk, 0.0).astype(BF16)
            u["ara_bd"] = jnp.where(masks["incl"], rblk, 0.0).astype(BF16)
            u["ark_anti"] = jnp.where(masks["anti_incl"], rblk, 0.0).astype(BF16)
            v_j, bt_j = u["v"], u["bt"]
            u["v_sw"] = jnp.concatenate([jnp.where(m0, z, v_j), jnp.where(m0, v_j, z)], 0).astype(BF16)
            u["b_st"] = jnp.concatenate([jnp.where(m0, bt_j, z), jnp.where(m0, z, bt_j)], 0).astype(BF16)
        for u in units:
            u["w2"] = _mm(u["abk_anti"], u["v_sw"]).astype(BF16)
        run_hook()

        eye = jnp.where(masks["eye"], 1.0, 0.0).astype(F32)
        for u in units:
            u["t"] = eye - jnp.where(masks["levels"][0], u["nmat"], 0.0)
        for li, lvl in enumerate(masks["levels"][1:]):
            blk = 2 << li
            if blk % 8 != 0:
                for u in units:
                    u["tb"] = u["t"].astype(BF16)
                    u["x"] = _mm(u["tb"], jnp.where(lvl, u["nmat"], 0.0).astype(BF16)).astype(BF16)
                for u in units:
                    u["t"] = u["t"] - _mm(u.pop("x"), u.pop("tb"))
            else:
                nblk = PAIR // blk
                for u in units:
                    u["tb"] = u["t"].astype(BF16)
                    t_odd = jnp.concatenate([u["t"][i * blk:(i + 1) * blk] for i in range(1, nblk, 2)], 0)
                    u["x"] = _mm(t_odd.astype(BF16), jnp.where(lvl, u["nmat"], 0.0).astype(BF16)).astype(BF16)
                for u in units:
                    y = _mm(u.pop("x"), u.pop("tb"))
                    t = u["t"]
                    u["t"] = jnp.concatenate(
                        [t[i * blk:(i + 1) * blk] - y[(i // 2) * blk:(i // 2 + 1) * blk] if i % 2
                         else t[i * blk:(i + 1) * blk] for i in range(nblk)], 0)
            if li in (0, 2, 4):
                run_hook()

        for u in units:
            u["tbv"] = _mm(u["t"].astype(BF16), jnp.concatenate([u["b_st"], u["w2"]], 1))
        zero_st = jnp.zeros((PAIR, PAIR), BF16)
        for u in units:
            tbv = u["tbv"]
            bp_st = tbv[:, :PAIR]
            vp_st = tbv[:, PAIR:]
            rhs_y = jnp.concatenate([
                jnp.concatenate([u["v_sw"], zero_st], 1),
                jnp.concatenate([(-vp_st).astype(BF16), (-bp_st).astype(BF16)], 1)], 0)
            u["yr"] = _mm(jnp.concatenate([u["ark_anti"], u["ara_bd"]], 1), rhs_y)
            bp = bp_st[:ch] + bp_st[ch:]
            vp = vp_st[:ch] + vp_st[ch:]
            khat = u["kt"] * u["pl_row"]
            ahat = u["at"] * u["pl_row"]
            zc = jnp.zeros((ch, PAIR), BF16)
            lhs_t = jnp.concatenate([u["v"], -vp, -ahat], 0).astype(BF16)
            rhs_t = jnp.concatenate([
                jnp.concatenate([khat.astype(BF16), zc], 1),
                jnp.concatenate([ahat.astype(BF16), zc], 1),
                jnp.concatenate([zc, bp.astype(BF16)], 1)], 0)
            u["gm"] = _mm_tn(lhs_t, rhs_t)
        run_hook()
        for ui, u in enumerate(units):
            yr, gm = u["yr"], u["gm"]
            y0_s[slot, ui] = yr[:ch, :PAIR] + yr[ch:, :PAIR]
            rp_s[slot, ui] = (u["rt"] + yr[:ch, PAIR:] + yr[ch:, PAIR:]).astype(BF16)
            gt_s[slot, ui] = jnp.where(masks["same"], gm[:, :PAIR], 0.0)
            m2_s[slot, ui] = jnp.where(masks["same"], gm[:, PAIR:], 0.0).astype(BF16)
            plr_s[slot, ui] = u["pl_row"]
        while hooks:
            run_hook()

    ngroups = seq // gl
    matmul_pass(0, [])

    def group(gi, carry):
        matmul_pass(gi, state_pass(gi - 1))
        return carry

    lax.fori_loop(1, ngroups, group, 0)
    for hook in state_pass(ngroups - 1):
        hook()


def _wkv(proj, lw, a, kk, ka, rk, gng, gnb, *, batch, seq_len):
    nq = lw.shape[0]
    bt = lw.shape[1]
    ns = WKV_SLABS
    npair = ns * SLAB // PAIR
    nunit = npair * WKV_GROUP
    tb = min(WKV_TIME_BLOCK, seq_len)
    nt = seq_len // tb
    pspec = lambda s: pl.BlockSpec((1, ns, tb, SLAB), lambda b, q, t, s=s: (s, q, b * nt + t, 0))
    aspec = pl.BlockSpec((ns, tb, SLAB), lambda b, q, t: (q, b * nt + t, 0))
    vspec = pl.BlockSpec((ns, 1, SLAB), lambda b, q, t: (q, 0, 0))
    return pl.pallas_call(
        _wkv_kernel,
        grid=(batch, nq // ns, nt),
        in_specs=[pspec(0), pspec(1), pspec(2), pspec(3), aspec, aspec, vspec, vspec, vspec, vspec, vspec],
        out_specs=aspec,
        out_shape=jax.ShapeDtypeStruct((nq, bt, SLAB), BF16),
        scratch_shapes=[
            pltpu.VMEM((npair, PAIR, PAIR), F32),
            pltpu.VMEM((2, nunit, WKV_CHUNK, PAIR), BF16),
            pltpu.VMEM((2, nunit, WKV_CHUNK, PAIR), F32),
            pltpu.VMEM((2, nunit, PAIR, PAIR), BF16),
            pltpu.VMEM((2, nunit, PAIR, PAIR), F32),
            pltpu.VMEM((2, nunit, 1, PAIR), F32),
            pltpu.VMEM((2, npair, WKV_GROUP * WKV_CHUNK, PAIR), F32),
        ],
        compiler_params=pltpu.CompilerParams(
            dimension_semantics=("arbitrary", "arbitrary", "arbitrary"),
            vmem_limit_bytes=VMEM_LIMIT),
    )(proj, proj, proj, proj, lw, a, kk, ka, rk, gng, gnb)


def _out_ln_kernel(y_ref, x_ref, w_ref, g_ref, b_ref, *o_refs):
    acc = _mm(y_ref[0], w_ref[0])
    for q in range(1, y_ref.shape[0]):
        acc = acc + _mm(y_ref[q], w_ref[q])
    h = DEEPNORM_ALPHA * x_ref[...] + acc
    mu = jnp.mean(h, axis=-1, keepdims=True)
    hc = h - mu
    var = jnp.mean(hc * hc, axis=-1, keepdims=True)
    out = hc * lax.rsqrt(var + LN_EPS) * g_ref[...] + b_ref[...]
    o_refs[0][...] = out
    if len(o_refs) > 1:
        o_refs[1][...] = out.astype(BF16)


def _out_ln(y_slab, x2d, w_slab, ln_g, ln_b, *, with_bf16):
    nq, bt, _ = y_slab.shape
    c = x2d.shape[1]
    tm = min(256, bt)
    row = pl.BlockSpec((tm, c), lambda i: (i, 0))
    out_specs = [row]
    out_shape = [jax.ShapeDtypeStruct((bt, c), F32)]
    if with_bf16:
        out_specs.append(row)
        out_shape.append(jax.ShapeDtypeStruct((bt, c), BF16))
    return pl.pallas_call(
        _out_ln_kernel,
        grid=(bt // tm,),
        in_specs=[
            pl.BlockSpec((nq, tm, SLAB), lambda i: (0, i, 0)),
            row,
            pl.BlockSpec((nq, SLAB, c), lambda i: (0, 0, 0)),
            pl.BlockSpec((1, c), lambda i: (0, 0)),
            pl.BlockSpec((1, c), lambda i: (0, 0)),
        ],
        out_specs=out_specs,
        out_shape=out_shape,
        compiler_params=pltpu.CompilerParams(
            dimension_semantics=("arbitrary",),
            vmem_limit_bytes=VMEM_LIMIT),
    )(y_slab, x2d, w_slab, ln_g, ln_b)


def _slab_matmul_kernel(x_ref, w_ref, o_ref, *, q_tiles):
    acc = _mm(x_ref[...], w_ref[...])
    acc = acc * jnp.where(pl.program_id(1) < q_tiles, DIFF_HEAD_DIM ** -0.5, 1.0)
    for q in range(o_ref.shape[0]):
        o_ref[q] = acc[:, q * SLAB:(q + 1) * SLAB].astype(BF16)


def _slab_matmul(x_bf, w_bf, *, q_cols):
    bt, c = x_bf.shape
    n_out = w_bf.shape[1]
    tm = min(1024, bt)
    tn = 1024
    return pl.pallas_call(
        functools.partial(_slab_matmul_kernel, q_tiles=q_cols // tn),
        grid=(bt // tm, n_out // tn),
        in_specs=[
            pl.BlockSpec((tm, c), lambda i, n: (i, 0)),
            pl.BlockSpec((c, tn), lambda i, n: (0, n)),
        ],
        out_specs=pl.BlockSpec((tn // SLAB, tm, SLAB), lambda i, n: (n, i, 0)),
        out_shape=jax.ShapeDtypeStruct((n_out // SLAB, bt, SLAB), BF16),
        compiler_params=pltpu.CompilerParams(
            dimension_semantics=("arbitrary", "arbitrary"),
            vmem_limit_bytes=VMEM_LIMIT),
    )(x_bf, w_bf)


def _attn_kernel(q_ref, k_ref, v_ref, g_ref, lam_ref, sg_ref, o_ref, m_ref, l_ref, acc_ref,
                 *, tq, lam_init):
    nh = q_ref.shape[0]
    hb = pl.program_id(1)
    qi = pl.program_id(2)
    dh = DIFF_HEAD_DIM
    slopes = [jnp.exp2(-(hb * nh + hh + 1).astype(F32)) for hh in range(nh)]
    lam_p = lam_ref[...]
    lam = (jnp.exp(jnp.sum(lam_p[0:1] * lam_p[1:2])) - jnp.exp(jnp.sum(lam_p[2:3] * lam_p[3:4]))
           + lam_init)

    ii = lax.broadcasted_iota(jnp.int32, (tq, tq), 0)
    jj = lax.broadcasted_iota(jnp.int32, (tq, tq), 1)
    dmat = (ii - jj).astype(F32)
    allowed = (jj // MASK_CHUNK) <= (ii // MASK_CHUNK)

    m_ref[...] = jnp.full(m_ref.shape, -jnp.inf, F32)
    l_ref[...] = jnp.zeros(l_ref.shape, F32)
    acc_ref[...] = jnp.zeros(acc_ref.shape, F32)

    nl = tq // 128
    streams = [(hh, u) for hh in range(nh) for u in range(2)]

    def step(blocks):
        nb = len(blocks)
        kbs = [[None] * nb for _ in range(nh)]
        vbs = [[None] * nb for _ in range(nh)]
        biases = [[None] * nb for _ in range(nh)]
        for b, (j, masked) in enumerate(blocks):
            k0 = pl.multiple_of(j * tq, tq)
            base = (k0 + ii).astype(F32) - jnp.abs(dmat) if masked else (k0 + jj[0:1, :]).astype(F32)
            for hh in range(nh):
                biases[hh][b] = slopes[hh] * base
                kbs[hh][b] = k_ref[hh, pl.ds(k0, tq), :]
                vbs[hh][b] = v_ref[hh, pl.ds(k0, tq), :]
        s = [[None] * nb for _ in streams]
        for w, (hh, u) in enumerate(streams):
            lanes = slice(u * dh, (u + 1) * dh)
            for b, (j, masked) in enumerate(blocks):
                su = _mm_nt(q_ref[hh, :, lanes], kbs[hh][b][:, lanes]) + biases[hh][b]
                if masked:
                    su = jnp.where(allowed, su, -jnp.inf)
                s[w][b] = su
        m_new, p = [], [[None] * nb for _ in streams]
        for w in range(len(streams)):
            pieces = [s[w][b][:, c * 128:(c + 1) * 128] for b in range(nb) for c in range(nl)]
            smax = functools.reduce(jnp.maximum, pieces)
            m_new.append(jnp.maximum(m_ref[w], jnp.max(smax, axis=1, keepdims=True)))
            mrep = jnp.concatenate([m_new[w]] * nl, axis=1)
            for b in range(nb):
                p[w][b] = jnp.exp(s[w][b] - mrep)
        pv = [sum(_mm(p[w][b].astype(BF16), vbs[hh][b]) for b in range(nb))
              for w, (hh, u) in enumerate(streams)]
        for w in range(len(streams)):
            pieces = [p[w][b][:, c * 128:(c + 1) * 128] for b in range(nb) for c in range(nl)]
            psum = jnp.sum(functools.reduce(jnp.add, pieces), axis=1, keepdims=True)
            corr = jnp.exp(m_ref[w] - m_new[w])
            l_ref[w] = corr * l_ref[w] + psum
            acc_ref[w] = jnp.concatenate([corr] * (SLAB // 128), axis=1) * acc_ref[w] + pv[w]
            m_ref[w] = m_new[w]

    def body(jp, carry):
        step([(2 * jp, False), (2 * jp + 1, False)])
        return carry

    lax.fori_loop(0, qi // 2, body, 0)

    @pl.when(qi % 2 == 1)
    def _():
        step([(qi - 1, False), (qi, True)])

    @pl.when(qi % 2 == 0)
    def _():
        step([(qi, True)])

    for hh in range(nh):
        inv_l = [1.0 / l_ref[2 * hh + u][:, 0:1] for u in range(2)]
        o = acc_ref[2 * hh] * inv_l[0] - lam * (acc_ref[2 * hh + 1] * inv_l[1])
        o = (o * lax.rsqrt(jnp.mean(o * o, axis=-1, keepdims=True) + SUBLN_EPS) * sg_ref[...]
             * (1.0 - lam_init))
        g = g_ref[hh].astype(F32)
        o_ref[hh] = (o * (g / (1.0 + jnp.exp(-g)))).astype(BF16)


def _diff_attention(qgkv, lam_p, subln_g, *, batch, seq_len, layer):
    nq = qgkv.shape[0] // 4
    bt = qgkv.shape[1]
    tq = min(256, seq_len)
    nh = ATTN_HEADS
    nhb = nq // nh
    nqb = seq_len // tq
    lam_init = 0.8 - 0.6 * math.exp(-0.3 * layer)
    kern = functools.partial(_attn_kernel, tq=tq, lam_init=lam_init)
    return pl.pallas_call(
        kern,
        grid=(batch, nhb, nqb),
        in_specs=[
            pl.BlockSpec((nh, tq, SLAB), lambda b, h, i: (h, b * nqb + i, 0)),
            pl.BlockSpec((nh, seq_len, SLAB), lambda b, h, i: (2 * nhb + h, b, 0)),
            pl.BlockSpec((nh, seq_len, SLAB), lambda b, h, i: (3 * nhb + h, b, 0)),
            pl.BlockSpec((nh, tq, SLAB), lambda b, h, i: (nhb + h, b * nqb + i, 0)),
            pl.BlockSpec(lam_p.shape, lambda b, h, i: (0, 0)),
            pl.BlockSpec((1, SLAB), lambda b, h, i: (0, 0)),
        ],
        out_specs=pl.BlockSpec((nh, tq, SLAB), lambda b, h, i: (h, b * nqb + i, 0)),
        out_shape=jax.ShapeDtypeStruct((nq, bt, SLAB), BF16),
        scratch_shapes=[pltpu.VMEM((2 * nh, tq, 128), F32), pltpu.VMEM((2 * nh, tq, 128), F32),
                        pltpu.VMEM((2 * nh, tq, SLAB), F32)],
        compiler_params=pltpu.CompilerParams(
            dimension_semantics=("arbitrary", "arbitrary", "arbitrary"),
            vmem_limit_bytes=VMEM_LIMIT),
    )(qgkv, qgkv, qgkv, qgkv, lam_p, subln_g)


def _pad_cols(w):
    return jnp.pad(w, ((0, 0), (0, LORA_PAD - w.shape[1])))


def _pad_rows(w):
    return jnp.pad(w, ((0, LORA_PAD - w.shape[0]), (0, 0)))


def kernel(x, a_mu_proj, a_mu_lora, a_w_in, a_w0, a_w1, a_w2, a_a0, a_a1, a_a2, a_k_k, a_k_a, a_r_k,
           a_gn_g, a_gn_b, a_w_out, w_k_shared, w_v_shared, b_w_qg, b_lambda, b_subln_g, b_w_out,
           ln_g, ln_b):
    batch, seq_len, c = x.shape
    assert a_w_in.shape[0] == 1 and b_w_qg.shape[0] == 1 and ln_g.shape[0] == DEPTH
    assert c % SLAB == 0 and seq_len % (WKV_CHUNK * WKV_GROUP) == 0
    bt = batch * seq_len
    nq = c // SLAB
    x2d = x.reshape(bt, c)
    slab_vec = lambda p: p.reshape(nq, 1, SLAB)
    slab_rows = lambda w: w.astype(BF16).reshape(nq, SLAB, w.shape[1])

    mu = jnp.concatenate([a_mu_proj[0], a_mu_lora[0]], axis=0)
    xs, lw, a = _rwkv_mix(
        x2d, mu,
        a_w0[0][None], _pad_cols(a_w1[0]).astype(BF16), _pad_rows(a_w2[0]).astype(BF16),
        a_a0[0][None], _pad_cols(a_a1[0]).astype(BF16), _pad_rows(a_a2[0]).astype(BF16),
        seq_len=seq_len)
    proj = _stream_matmul(xs, a_w_in[0].astype(BF16))
    yg = _wkv(proj, lw, a, slab_vec(a_k_k[0]), slab_vec(a_k_a[0]), slab_vec(a_r_k[0]),
              slab_vec(a_gn_g[0]), slab_vec(a_gn_b[0]), batch=batch, seq_len=seq_len)
    x1, x1_bf = _out_ln(yg, x2d, slab_rows(a_w_out[0]), ln_g[0][None], ln_b[0][None], with_bf16=True)

    w_all = jnp.concatenate([b_w_qg[0], w_k_shared, w_v_shared], axis=1).astype(BF16)
    qgkv = _slab_matmul(x1_bf, w_all, q_cols=c)
    og = _diff_attention(qgkv, b_lambda[0], b_subln_g[0][None], batch=batch, seq_len=seq_len, layer=1)
    (out,) = _out_ln(og, x1, slab_rows(b_w_out[0]), ln_g[1][None], ln_b[1][None], with_bf16=False)
    return out.reshape(batch, seq_len, c)
```

```python
import functools
import math

import jax
import jax.numpy as jnp
from jax import lax
from jax.experimental import pallas as pl
from jax.experimental.pallas import tpu as pltpu

F32 = jnp.float32
BF16 = jnp.bfloat16

DEPTH = 2
RWKV_HEAD = 64
MASK_CHUNK = 64
DIFF_HEAD_DIM = 128
LOG2E = math.log2(math.e)
GN_EPS = 64e-5
SUBLN_EPS = 1e-5
LN_EPS = 1e-5
DEEPNORM_ALPHA = (2.0 * DEPTH) ** 0.25

SLAB = 256
PAIR = 128
LORA_PAD = 128
WKV_CHUNK = 64
WKV_GROUP = 4
WKV_SLABS = 2
WKV_TIME_BLOCK = 2048
ATTN_HEADS = 4
VMEM_LIMIT = 56 * 1024 * 1024


def _mm(a, b):
    return jnp.dot(a, b, preferred_element_type=F32)


def _mm_nt(a, b):
    return lax.dot_general(a, b, (((1,), (1,)), ((), ())), preferred_element_type=F32)


def _mm_tn(a, b):
    return lax.dot_general(a, b, (((0,), (0,)), ((), ())), preferred_element_type=F32)


def _split2(x):
    hi = x.astype(BF16)
    lo = (x - hi.astype(F32)).astype(BF16)
    return hi, lo


def _rwkv_mix_kernel(x_ref, xp_ref, mu_ref, w0_ref, w1_ref, w2_ref, a0_ref, a1_ref, a2_ref,
                     xs_ref, lw_ref, a_ref, *, seq_tiles):
    i = pl.program_id(0)
    x = x_ref[...]
    prev_last = xp_ref[7:8, :]
    prev_last = jnp.where((i % seq_tiles) == 0, 0.0, prev_last)
    row = lax.broadcasted_iota(jnp.int32, x.shape, 0)
    xsh = jnp.where(row == 0, prev_last, pltpu.roll(x, 1, axis=0))
    xx = xsh - x
    for st in range(4):
        xs_ref[st] = (x + xx * mu_ref[st:st + 1, :]).astype(BF16)
    xw = (x + xx * mu_ref[4:5, :]).astype(BF16)
    xa = (x + xx * mu_ref[5:6, :]).astype(BF16)
    hw = jnp.tanh(_mm(xw, w1_ref[...]))
    z = w0_ref[...] + _mm(hw.astype(BF16), w2_ref[...])
    sp = jnp.maximum(-z, 0.0) + jnp.log(1.0 + jnp.exp(-jnp.abs(z)))
    lw = -jnp.exp(-sp - 0.5)
    ha = _mm(xa, a1_ref[...])
    za = a0_ref[...] + _mm(ha.astype(BF16), a2_ref[...])
    a = 1.0 / (1.0 + jnp.exp(-za))
    for q in range(lw_ref.shape[0]):
        lw_ref[q] = lw[:, q * SLAB:(q + 1) * SLAB]
        a_ref[q] = a[:, q * SLAB:(q + 1) * SLAB].astype(BF16)


def _rwkv_mix(x2d, mu, w0, w1, w2, a0, a1, a2, *, seq_len):
    bt, c = x2d.shape
    tm = min(256, seq_len)
    nq = c // SLAB
    kern = functools.partial(_rwkv_mix_kernel, seq_tiles=seq_len // tm)
    full = lambda shape: pl.BlockSpec(shape, lambda i: (0,) * len(shape))
    return pl.pallas_call(
        kern,
        grid=(bt // tm,),
        in_specs=[
            pl.BlockSpec((tm, c), lambda i: (i, 0)),
            pl.BlockSpec((8, c), lambda i: (jnp.maximum(i * (tm // 8) - 1, 0), 0)),
            full((6, c)),
            full((1, c)), full((c, LORA_PAD)), full((LORA_PAD, c)),
            full((1, c)), full((c, LORA_PAD)), full((LORA_PAD, c)),
        ],
        out_specs=[
            pl.BlockSpec((4, tm, c), lambda i: (0, i, 0)),
            pl.BlockSpec((nq, tm, SLAB), lambda i: (0, i, 0)),
            pl.BlockSpec((nq, tm, SLAB), lambda i: (0, i, 0)),
        ],
        out_shape=[
            jax.ShapeDtypeStruct((4, bt, c), BF16),
            jax.ShapeDtypeStruct((nq, bt, SLAB), F32),
            jax.ShapeDtypeStruct((nq, bt, SLAB), BF16),
        ],
        compiler_params=pltpu.CompilerParams(
            dimension_semantics=("arbitrary",),
            vmem_limit_bytes=VMEM_LIMIT),
    )(x2d, x2d, mu, w0, w1, w2, a0, a1, a2)


def _stream_matmul_kernel(x_ref, w_ref, o_ref):
    acc = _mm(x_ref[0], w_ref[0])
    for q in range(o_ref.shape[1]):
        o_ref[0, q] = acc[:, q * SLAB:(q + 1) * SLAB].astype(BF16)


def _stream_matmul(xs, w_in):
    ns, bt, c = xs.shape
    tm = min(1024, bt)
    tn = 1024
    return pl.pallas_call(
        _stream_matmul_kernel,
        grid=(bt // tm, ns, c // tn),
        in_specs=[
            pl.BlockSpec((1, tm, c), lambda i, s, n: (s, i, 0)),
            pl.BlockSpec((1, c, tn), lambda i, s, n: (s, 0, n)),
        ],
        out_specs=pl.BlockSpec((1, tn // SLAB, tm, SLAB), lambda i, s, n: (s, n, i, 0)),
        out_shape=jax.ShapeDtypeStruct((ns, c // SLAB, bt, SLAB), BF16),
        compiler_params=pltpu.CompilerParams(
            dimension_semantics=("arbitrary", "arbitrary", "arbitrary"),
            vmem_limit_bytes=VMEM_LIMIT),
    )(xs, w_in)


def _wkv_masks():
    ti = lax.broadcasted_iota(jnp.int32, (PAIR, PAIR), 0)
    si = lax.broadcasted_iota(jnp.int32, (PAIR, PAIR), 1)
    same = (ti // WKV_CHUNK) == (si // WKV_CHUNK)
    tl = ti % WKV_CHUNK
    sl = si % WKV_CHUNK
    levels = []
    b = 1
    while b < WKV_CHUNK:
        levels.append(same & ((tl // b) == (sl // b) + 1) & (((tl // b) % 2) == 1))
        b *= 2
    return dict(same=same, eye=ti == si,
                strict=same & (sl < tl), incl=same & (sl <= tl),
                anti_strict=(~same) & (sl < tl), anti_incl=(~same) & (sl <= tl),
                levels=levels)


def _wkv_kernel(r_ref, k_ref, v_ref, g_ref, lw_ref, a_ref, kk_ref, ka_ref, rk_ref, gng_ref, gnb_ref,
                o_ref, h_ref, rp_s, y0_s, m2_s, gt_s, plr_s, bonus_s):
    seq = o_ref.shape[1]
    ch = WKV_CHUNK
    gl = WKV_GROUP * ch
    masks = _wkv_masks()
    onesbd = jnp.where(masks["same"], 1.0, 0.0).astype(BF16)
    lane = lax.broadcasted_iota(jnp.int32, (ch, PAIR), 1)
    m0 = lane < RWKV_HEAD
    gt = lax.broadcasted_iota(jnp.int32, (gl, gl), 0)
    gs = lax.broadcasted_iota(jnp.int32, (gl, gl), 1)
    tri = jnp.where(((gt // ch) == (gs // ch)) & (gs <= gt), 1.0, 0.0).astype(BF16)

    glane = lax.broadcasted_iota(jnp.int32, (gl, PAIR), 1)
    g0 = glane < RWKV_HEAD

    def head_sum(x):
        s0 = jnp.sum(jnp.where(g0, x, 0.0), axis=1, keepdims=True)
        s1 = jnp.sum(jnp.where(g0, 0.0, x), axis=1, keepdims=True)
        return jnp.where(g0, s0, s1)

    @pl.when(pl.program_id(2) == 0)
    def _():
        h_ref[...] = jnp.zeros(h_ref.shape, F32)

    pps = SLAB // PAIR
    npair = o_ref.shape[0] * pps
    inv_n = 1.0 / RWKV_HEAD

    nunit = npair * WKV_GROUP

    def rows_of(gi):
        return pl.ds(pl.multiple_of(gi * gl, gl), gl)

    def state_pass(gi):
        slot = gi % 2
        rows = rows_of(gi)
        st = dict(hts=None, ys=[[] for _ in range(npair)])

        def chunk_step(j):
            if j == 0:
                st["hts"] = [h_ref[p] for p in range(npair)]
            for p in range(npair):
                ui = p * WKV_GROUP + j
                htb = st["hts"][p].astype(BF16)
                st["ys"][p].append(_mm_nt(rp_s[slot, ui], htb) + y0_s[slot, ui])
                st["hts"][p] = st["hts"][p] * plr_s[slot, ui] + _mm_nt(htb, m2_s[slot, ui]) + gt_s[slot, ui]
            if j == WKV_GROUP - 1:
                for p in range(npair):
                    h_ref[p] = st["hts"][p]

        def finish():
            for p in range(npair):
                sb = p // pps
                lanes = slice((p % pps) * PAIR, (p % pps + 1) * PAIR)
                y = jnp.concatenate(st["ys"][p], 0)
                mean = head_sum(y) * inv_n
                yc = y - mean
                var = head_sum(yc * yc) * inv_n
                yn = yc * lax.rsqrt(var + GN_EPS) * gng_ref[sb, :, lanes] + gnb_ref[sb, :, lanes]
                g = g_ref[0, sb, rows, lanes].astype(F32)
                out = (yn + bonus_s[slot, p]) * (g / (1.0 + jnp.exp(-g)))
                o_ref[sb, rows, lanes] = out.astype(BF16)

        return [functools.partial(chunk_step, j) for j in range(WKV_GROUP)] + [finish]

    def matmul_pass(gi, hooks):
        slot = gi % 2
        rows = rows_of(gi)
        hooks = list(hooks)

        def run_hook():
            if hooks:
                hooks.pop(0)()

        pre = []
        for p in range(npair):
            sb = p // pps
            lanes = slice((p % pps) * PAIR, (p % pps + 1) * PAIR)
            r = r_ref[0, sb, rows, lanes].astype(F32)
            k = k_ref[0, sb, rows, lanes].astype(F32)
            v = v_ref[0, sb, rows, lanes].astype(F32)
            lw = lw_ref[sb, rows, lanes]
            a = a_ref[sb, rows, lanes].astype(F32)
            l1, l2 = _split2(lw)
            c = _mm(tri, l1) + _mm(tri, l2)
            pw = jnp.exp(c)
            pinv = jnp.exp(-c)
            pex = jnp.exp(c - lw)
            kkr = k * kk_ref[sb, :, lanes]
            kk = kkr * lax.rsqrt(jnp.maximum(head_sum(kkr * kkr), 1e-24))
            kmod = k * (1.0 + (a - 1.0) * ka_ref[sb, :, lanes])
            bonus = head_sum(r * kmod * rk_ref[sb, :, lanes]) * v
            bonus_s[slot, p] = bonus
            pre.append(dict(v=v, pw=pw, rt=r * pw, bt=kk * pex, kt=kmod * pinv, at=a * kk * pinv))

        units = []
        for p in range(npair):
            d = pre[p]
            for j in range(WKV_GROUP):
                cs = slice(j * ch, (j + 1) * ch)
                units.append(dict(rt=d["rt"][cs], bt=d["bt"][cs], kt=d["kt"][cs], at=d["at"][cs],
                                  v=d["v"][cs], pl_row=d["pw"][j * ch + ch - 1:j * ch + ch, :]))

        z = jnp.zeros((ch, PAIR), F32)
        for u in units:
            rt_j, bt_j, kt_j, at_j = u["rt"], u["bt"], u["kt"], u["at"]
            lhs0 = jnp.concatenate([jnp.where(m0, rt_j, z), jnp.where(m0, bt_j, z)], 0).astype(BF16)
            lhs1 = jnp.concatenate([jnp.where(m0, z, rt_j), jnp.where(m0, z, bt_j)], 0).astype(BF16)
            rhs0 = jnp.concatenate([at_j, kt_j], 0).astype(BF16)
            rhs1 = jnp.concatenate([kt_j, at_j], 0).astype(BF16)
            u["a0"] = _mm_nt(lhs0, rhs0)
            u["a1"] = _mm_nt(lhs1, rhs1)
        for u in units:
            a0, a1 = u.pop("a0"), u.pop("a1")
            rblk = jnp.concatenate([a0[:ch], a1[:ch]], 0)
            bblk = jnp.concatenate([a0[ch:], a1[ch:]], 0)
            u["nmat"] = jnp.where(masks["strict"], bblk, 0.0)
            u["abk_anti"] = jnp.where(masks["anti_strict"], bblk, 0.0).astype(BF16)
            u["ara_bd"] = jnp.where(masks["incl"], rblk, 0.0).astype(BF16)
            u["ark_anti"] = jnp.where(masks["anti_incl"], rblk, 0.0).astype(BF16)
            v_j, bt_j = u["v"], u["bt"]
            u["v_sw"] = jnp.concatenate([jnp.where(m0, z, v_j), jnp.where(m0, v_j, z)], 0).astype(BF16)
            u["b_st"] = jnp.concatenate([jnp.where(m0, bt_j, z), jnp.where(m0, z, bt_j)], 0).astype(BF16)
        for u in units:
            u["w2"] = _mm(u["abk_anti"], u["v_sw"]).astype(BF16)
        run_hook()

        eye = jnp.where(masks["eye"], 1.0, 0.0).astype(F32)
        for u in units:
            u["t"] = eye - jnp.where(masks["levels"][0], u["nmat"], 0.0)
        for li, lvl in enumerate(masks["levels"][1:]):
            blk = 2 << li
            if blk % 8 != 0:
                for u in units:
                    u["tb"] = u["t"].astype(BF16)
                    u["x"] = _mm(u["tb"], jnp.where(lvl, u["nmat"], 0.0).astype(BF16)).astype(BF16)
                for u in units:
                    u["t"] = u["t"] - _mm(u.pop("x"), u.pop("tb"))
            else:
                nblk = PAIR // blk
                for u in units:
                    u["tb"] = u["t"].astype(BF16)
                    t_odd = jnp.concatenate([u["t"][i * blk:(i + 1) * blk] for i in range(1, nblk, 2)], 0)
                    u["x"] = _mm(t_odd.astype(BF16), jnp.where(lvl, u["nmat"], 0.0).astype(BF16)).astype(BF16)
                for u in units:
                    y = _mm(u.pop("x"), u.pop("tb"))
                    t = u["t"]
                    u["t"] = jnp.concatenate(
                        [t[i * blk:(i + 1) * blk] - y[(i // 2) * blk:(i // 2 + 1) * blk] if i % 2
                         else t[i * blk:(i + 1) * blk] for i in range(nblk)], 0)
            if li in (0, 2, 4):
                run_hook()

        for u in units:
            u["tbv"] = _mm(u["t"].astype(BF16), jnp.concatenate([u["b_st"], u["w2"]], 1))
        zero_st = jnp.zeros((PAIR, PAIR), BF16)
        for u in units:
            tbv = u["tbv"]
            bp_st = tbv[:, :PAIR]
            vp_st = tbv[:, PAIR:]
            rhs_y = jnp.concatenate([
                jnp.concatenate([u["v_sw"], zero_st], 1),
                jnp.concatenate([(-vp_st).astype(BF16), (-bp_st).astype(BF16)], 1)], 0)
            u["yr"] = _mm(jnp.concatenate([u["ark_anti"], u["ara_bd"]], 1), rhs_y)
            bp = bp_st[:ch] + bp_st[ch:]
            vp = vp_st[:ch] + vp_st[ch:]
            khat = u["kt"] * u["pl_row"]
            ahat = u["at"] * u["pl_row"]
            zc = jnp.zeros((ch, PAIR), BF16)
            lhs_t = jnp.concatenate([u["v"], -vp, -ahat], 0).astype(BF16)
            rhs_t = jnp.concatenate([
                jnp.concatenate([khat.astype(BF16), zc], 1),
                jnp.concatenate([ahat.astype(BF16), zc], 1),
                jnp.concatenate([zc, bp.astype(BF16)], 1)], 0)
            u["gm"] = _mm_tn(lhs_t, rhs_t)
        run_hook()
        for ui, u in enumerate(units):
            yr, gm = u["yr"], u["gm"]
            y0_s[slot, ui] = yr[:ch, :PAIR] + yr[ch:, :PAIR]
            rp_s[slot, ui] = (u["rt"] + yr[:ch, PAIR:] + yr[ch:, PAIR:]).astype(BF16)
            gt_s[slot, ui] = jnp.where(masks["same"], gm[:, :PAIR], 0.0)
            m2_s[slot, ui] = jnp.where(masks["same"], gm[:, PAIR:], 0.0).astype(BF16)
            plr_s[slot, ui] = u["pl_row"]
        while hooks:
            run_hook()

    ngroups = seq // gl
    matmul_pass(0, [])

    def group(gi, carry):
        matmul_pass(gi, state_pass(gi - 1))
        return carry

    lax.fori_loop(1, ngroups, group, 0)
    for hook in state_pass(ngroups - 1):
        hook()


def _wkv(proj, lw, a, kk, ka, rk, gng, gnb, *, batch, seq_len):
    nq = lw.shape[0]
    bt = lw.shape[1]
    ns = WKV_SLABS
    npair = ns * SLAB // PAIR
    nunit = npair * WKV_GROUP
    tb = min(WKV_TIME_BLOCK, seq_len)
    nt = seq_len // tb
    pspec = lambda s: pl.BlockSpec((1, ns, tb, SLAB), lambda b, q, t, s=s: (s, q, b * nt + t, 0))
    aspec = pl.BlockSpec((ns, tb, SLAB), lambda b, q, t: (q, b * nt + t, 0))
    vspec = pl.BlockSpec((ns, 1, SLAB), lambda b, q, t: (q, 0, 0))
    return pl.pallas_call(
        _wkv_kernel,
        grid=(batch, nq // ns, nt),
        in_specs=[pspec(0), pspec(1), pspec(2), pspec(3), aspec, aspec, vspec, vspec, vspec, vspec, vspec],
        out_specs=aspec,
        out_shape=jax.ShapeDtypeStruct((nq, bt, SLAB), BF16),
        scratch_shapes=[
            pltpu.VMEM((npair, PAIR, PAIR), F32),
            pltpu.VMEM((2, nunit, WKV_CHUNK, PAIR), BF16),
            pltpu.VMEM((2, nunit, WKV_CHUNK, PAIR), F32),
            pltpu.VMEM((2, nunit, PAIR, PAIR), BF16),
            pltpu.VMEM((2, nunit, PAIR, PAIR), F32),
            pltpu.VMEM((2, nunit, 1, PAIR), F32),
            pltpu.VMEM((2, npair, WKV_GROUP * WKV_CHUNK, PAIR), F32),
        ],
        compiler_params=pltpu.CompilerParams(
            dimension_semantics=("arbitrary", "arbitrary", "arbitrary"),
            vmem_limit_bytes=VMEM_LIMIT),
    )(proj, proj, proj, proj, lw, a, kk, ka, rk, gng, gnb)


def _out_ln_kernel(y_ref, x_ref, w_ref, g_ref, b_ref, *o_refs):
    acc = _mm(y_ref[0], w_ref[0])
    for q in range(1, y_ref.shape[0]):
        acc = acc + _mm(y_ref[q], w_ref[q])
    h = DEEPNORM_ALPHA * x_ref[...] + acc
    mu = jnp.mean(h, axis=-1, keepdims=True)
    hc = h - mu
    var = jnp.mean(hc * hc, axis=-1, keepdims=True)
    out = hc * lax.rsqrt(var + LN_EPS) * g_ref[...] + b_ref[...]
    o_refs[0][...] = out
    if len(o_refs) > 1:
        o_refs[1][...] = out.astype(BF16)


def _out_ln(y_slab, x2d, w_slab, ln_g, ln_b, *, with_bf16):
    nq, bt, _ = y_slab.shape
    c = x2d.shape[1]
    tm = min(256, bt)
    row = pl.BlockSpec((tm, c), lambda i: (i, 0))
    out_specs = [row]
    out_shape = [jax.ShapeDtypeStruct((bt, c), F32)]
    if with_bf16:
        out_specs.append(row)
        out_shape.append(jax.ShapeDtypeStruct((bt, c), BF16))
    return pl.pallas_call(
        _out_ln_kernel,
        grid=(bt // tm,),
        in_specs=[
            pl.BlockSpec((nq, tm, SLAB), lambda i: (0, i, 0)),
            row,
            pl.BlockSpec((nq, SLAB, c), lambda i: (0, 0, 0)),
            pl.BlockSpec((1, c), lambda i: (0, 0)),
            pl.BlockSpec((1, c), lambda i: (0, 0)),
        ],
        out_specs=out_specs,
        out_shape=out_shape,
        compiler_params=pltpu.CompilerParams(
            dimension_semantics=("arbitrary",),
            vmem_limit_bytes=VMEM_LIMIT),
    )(y_slab, x2d, w_slab, ln_g, ln_b)


def _slab_matmul_kernel(x_ref, w_ref, o_ref, *, q_tiles):
    acc = _mm(x_ref[...], w_ref[...])
    acc = acc * jnp.where(pl.program_id(1) < q_tiles, DIFF_HEAD_DIM ** -0.5 * LOG2E, 1.0)
    for q in range(o_ref.shape[0]):
        o_ref[q] = acc[:, q * SLAB:(q + 1) * SLAB].astype(BF16)


def _slab_matmul(x_bf, w_bf, *, q_cols):
    bt, c = x_bf.shape
    n_out = w_bf.shape[1]
    tm = min(1024, bt)
    tn = 1024
    return pl.pallas_call(
        functools.partial(_slab_matmul_kernel, q_tiles=q_cols // tn),
        grid=(bt // tm, n_out // tn),
        in_specs=[
            pl.BlockSpec((tm, c), lambda i, n: (i, 0)),
            pl.BlockSpec((c, tn), lambda i, n: (0, n)),
        ],
        out_specs=pl.BlockSpec((tn // SLAB, tm, SLAB), lambda i, n: (n, i, 0)),
        out_shape=jax.ShapeDtypeStruct((n_out // SLAB, bt, SLAB), BF16),
        compiler_params=pltpu.CompilerParams(
            dimension_semantics=("arbitrary", "arbitrary"),
            vmem_limit_bytes=VMEM_LIMIT),
    )(x_bf, w_bf)


def _attn_kernel(q_ref, k_ref, v_ref, g_ref, lam_ref, sg_ref, o_ref, m_ref, l_ref, acc_ref,
                 *, tq, lam_init):
    nh = q_ref.shape[0]
    hb = pl.program_id(1)
    qi = pl.program_id(2)
    dh = DIFF_HEAD_DIM
    slopes = [jnp.exp2(-(hb * nh + hh + 1).astype(F32)) * LOG2E for hh in range(nh)]
    lam_p = lam_ref[...]
    lam = (jnp.exp(jnp.sum(lam_p[0:1] * lam_p[1:2])) - jnp.exp(jnp.sum(lam_p[2:3] * lam_p[3:4]))
           + lam_init)

    ii = lax.broadcasted_iota(jnp.int32, (tq, tq), 0)
    jj = lax.broadcasted_iota(jnp.int32, (tq, tq), 1)
    dmat = (ii - jj).astype(F32)
    allowed = (jj // MASK_CHUNK) <= (ii // MASK_CHUNK)

    m_ref[...] = jnp.full(m_ref.shape, -jnp.inf, F32)
    l_ref[...] = jnp.zeros(l_ref.shape, F32)
    acc_ref[...] = jnp.zeros(acc_ref.shape, F32)

    nl = tq // 128
    streams = [(hh, u) for hh in range(nh) for u in range(2)]

    def step(blocks):
        nb = len(blocks)
        kbs = [[None] * nb for _ in range(nh)]
        vbs = [[None] * nb for _ in range(nh)]
        biases = [[None] * nb for _ in range(nh)]
        for b, (j, masked) in enumerate(blocks):
            k0 = pl.multiple_of(j * tq, tq)
            base = (k0 + ii).astype(F32) - jnp.abs(dmat) if masked else (k0 + jj[0:1, :]).astype(F32)
            for hh in range(nh):
                biases[hh][b] = slopes[hh] * base
                kbs[hh][b] = k_ref[hh, pl.ds(k0, tq), :]
                vbs[hh][b] = v_ref[hh, pl.ds(k0, tq), :]
        s = [[None] * nb for _ in streams]
        for w, (hh, u) in enumerate(streams):
            lanes = slice(u * dh, (u + 1) * dh)
            for b, (j, masked) in enumerate(blocks):
                su = _mm_nt(q_ref[hh, :, lanes], kbs[hh][b][:, lanes]) + biases[hh][b]
                if masked:
                    su = jnp.where(allowed, su, -jnp.inf)
                s[w][b] = su
        m_new, p = [], [[None] * nb for _ in streams]
        for w in range(len(streams)):
            pieces = [s[w][b][:, c * 128:(c + 1) * 128] for b in range(nb) for c in range(nl)]
            smax = functools.reduce(jnp.maximum, pieces)
            m_new.append(jnp.maximum(m_ref[w], jnp.max(smax, axis=1, keepdims=True)))
            mrep = jnp.concatenate([m_new[w]] * nl, axis=1)
            for b in range(nb):
                p[w][b] = jnp.exp2(s[w][b] - mrep)
        pv = [sum(_mm(p[w][b].astype(BF16), vbs[hh][b]) for b in range(nb))
              for w, (hh, u) in enumerate(streams)]
        for w in range(len(streams)):
            pieces = [p[w][b][:, c * 128:(c + 1) * 128] for b in range(nb) for c in range(nl)]
            psum = jnp.sum(functools.reduce(jnp.add, pieces), axis=1, keepdims=True)
            corr = jnp.exp2(m_ref[w] - m_new[w])
            l_ref[w] = corr * l_ref[w] + psum
            acc_ref[w] = jnp.concatenate([corr] * (SLAB // 128), axis=1) * acc_ref[w] + pv[w]
            m_ref[w] = m_new[w]

    def body(jp, carry):
        step([(2 * jp, False), (2 * jp + 1, False)])
        return carry

    lax.fori_loop(0, qi // 2, body, 0)

    @pl.when(qi % 2 == 1)
    def _():
        step([(qi - 1, False), (qi, True)])

    @pl.when(qi % 2 == 0)
    def _():
        step([(qi, True)])

    for hh in range(nh):
        inv_l = [1.0 / l_ref[2 * hh + u][:, 0:1] for u in range(2)]
        o = acc_ref[2 * hh] * inv_l[0] - lam * (acc_ref[2 * hh + 1] * inv_l[1])
        o = (o * lax.rsqrt(jnp.mean(o * o, axis=-1, keepdims=True) + SUBLN_EPS) * sg_ref[...]
             * (1.0 - lam_init))
        g = g_ref[hh].astype(F32)
        o_ref[hh] = (o * (g / (1.0 + jnp.exp(-g)))).astype(BF16)


def _diff_attention(qgkv, lam_p, subln_g, *, batch, seq_len, layer):
    nq = qgkv.shape[0] // 4
    bt = qgkv.shape[1]
    tq = min(256, seq_len)
    nh = ATTN_HEADS
    nhb = nq // nh
    nqb = seq_len // tq
    lam_init = 0.8 - 0.6 * math.exp(-0.3 * layer)
    kern = functools.partial(_attn_kernel, tq=tq, lam_init=lam_init)
    return pl.pallas_call(
        kern,
        grid=(batch, nhb, nqb),
        in_specs=[
            pl.BlockSpec((nh, tq, SLAB), lambda b, h, i: (h, b * nqb + i, 0)),
            pl.BlockSpec((nh, seq_len, SLAB), lambda b, h, i: (2 * nhb + h, b, 0)),
            pl.BlockSpec((nh, seq_len, SLAB), lambda b, h, i: (3 * nhb + h, b, 0)),
            pl.BlockSpec((nh, tq, SLAB), lambda b, h, i: (nhb + h, b * nqb + i, 0)),
            pl.BlockSpec(lam_p.shape, lambda b, h, i: (0, 0)),
            pl.BlockSpec((1, SLAB), lambda b, h, i: (0, 0)),
        ],
        out_specs=pl.BlockSpec((nh, tq, SLAB), lambda b, h, i: (h, b * nqb + i, 0)),
        out_shape=jax.ShapeDtypeStruct((nq, bt, SLAB), BF16),
        scratch_shapes=[pltpu.VMEM((2 * nh, tq, 128), F32), pltpu.VMEM((2 * nh, tq, 128), F32),
                        pltpu.VMEM((2 * nh, tq, SLAB), F32)],
        compiler_params=pltpu.CompilerParams(
            dimension_semantics=("arbitrary", "arbitrary", "arbitrary"),
            vmem_limit_bytes=VMEM_LIMIT),
    )(qgkv, qgkv, qgkv, qgkv, lam_p, subln_g)


def _pad_cols(w):
    return jnp.pad(w, ((0, 0), (0, LORA_PAD - w.shape[1])))


def _pad_rows(w):
    return jnp.pad(w, ((0, LORA_PAD - w.shape[0]), (0, 0)))


def kernel(x, a_mu_proj, a_mu_lora, a_w_in, a_w0, a_w1, a_w2, a_a0, a_a1, a_a2, a_k_k, a_k_a, a_r_k,
           a_gn_g, a_gn_b, a_w_out, w_k_shared, w_v_shared, b_w_qg, b_lambda, b_subln_g, b_w_out,
           ln_g, ln_b):
    batch, seq_len, c = x.shape
    assert a_w_in.shape[0] == 1 and b_w_qg.shape[0] == 1 and ln_g.shape[0] == DEPTH
    assert c % SLAB == 0 and seq_len % (WKV_CHUNK * WKV_GROUP) == 0
    bt = batch * seq_len
    nq = c // SLAB
    x2d = x.reshape(bt, c)
    slab_vec = lambda p: p.reshape(nq, 1, SLAB)
    slab_rows = lambda w: w.astype(BF16).reshape(nq, SLAB, w.shape[1])

    mu = jnp.concatenate([a_mu_proj[0], a_mu_lora[0]], axis=0)
    xs, lw, a = _rwkv_mix(
        x2d, mu,
        a_w0[0][None], _pad_cols(a_w1[0]).astype(BF16), _pad_rows(a_w2[0]).astype(BF16),
        a_a0[0][None], _pad_cols(a_a1[0]).astype(BF16), _pad_rows(a_a2[0]).astype(BF16),
        seq_len=seq_len)
    proj = _stream_matmul(xs, a_w_in[0].astype(BF16))
    yg = _wkv(proj, lw, a, slab_vec(a_k_k[0]), slab_vec(a_k_a[0]), slab_vec(a_r_k[0]),
              slab_vec(a_gn_g[0]), slab_vec(a_gn_b[0]), batch=batch, seq_len=seq_len)
    x1, x1_bf = _out_ln(yg, x2d, slab_rows(a_w_out[0]), ln_g[0][None], ln_b[0][None], with_bf16=True)

    w_all = jnp.concatenate([b_w_qg[0], w_k_shared, w_v_shared], axis=1).astype(BF16)
    qgkv = _slab_matmul(x1_bf, w_all, q_cols=c)
    og = _diff_attention(qgkv, b_lambda[0], b_subln_g[0][None], batch=batch, seq_len=seq_len, layer=1)
    (out,) = _out_ln(og, x1, slab_rows(b_w_out[0]), ln_g[1][None], ln_b[1][None], with_bf16=False)
    return out.reshape(batch, seq_len, c)
```

```python
import functools
import math

import jax
import jax.numpy as jnp
from jax import lax
from jax.experimental import pallas as pl
from jax.experimental.pallas import tpu as pltpu

F32 = jnp.float32
BF16 = jnp.bfloat16

DEPTH = 2
RWKV_HEAD = 64
MASK_CHUNK = 64
DIFF_HEAD_DIM = 128
LOG2E = math.log2(math.e)
GN_EPS = 64e-5
SUBLN_EPS = 1e-5
LN_EPS = 1e-5
DEEPNORM_ALPHA = (2.0 * DEPTH) ** 0.25

SLAB = 256
PAIR = 128
LORA_PAD = 128
WKV_CHUNK = 64
WKV_GROUP = 4
WKV_SLABS = 2
WKV_TIME_BLOCK = 2048
ATTN_HEADS = 4
VMEM_LIMIT = 56 * 1024 * 1024


def _mm(a, b):
    return jnp.dot(a, b, preferred_element_type=F32)


def _mm_nt(a, b):
    return lax.dot_general(a, b, (((1,), (1,)), ((), ())), preferred_element_type=F32)


def _mm_tn(a, b):
    return lax.dot_general(a, b, (((0,), (0,)), ((), ())), preferred_element_type=F32)


def _split2(x):
    hi = x.astype(BF16)
    lo = (x - hi.astype(F32)).astype(BF16)
    return hi, lo


def _rwkv_mix_kernel(x_ref, xp_ref, mu_ref, w0_ref, w1_ref, w2_ref, a0_ref, a1_ref, a2_ref,
                     xs_ref, lw_ref, a_ref, *, seq_tiles):
    i = pl.program_id(0)
    x = x_ref[...]
    prev_last = xp_ref[7:8, :]
    prev_last = jnp.where((i % seq_tiles) == 0, 0.0, prev_last)
    row = lax.broadcasted_iota(jnp.int32, x.shape, 0)
    xsh = jnp.where(row == 0, prev_last, pltpu.roll(x, 1, axis=0))
    xx = xsh - x
    for st in range(4):
        xs_ref[st] = (x + xx * mu_ref[st:st + 1, :]).astype(BF16)
    xw = (x + xx * mu_ref[4:5, :]).astype(BF16)
    xa = (x + xx * mu_ref[5:6, :]).astype(BF16)
    hw = jnp.tanh(_mm(xw, w1_ref[...]))
    z = w0_ref[...] + _mm(hw.astype(BF16), w2_ref[...])
    sp = jnp.maximum(-z, 0.0) + jnp.log(1.0 + jnp.exp(-jnp.abs(z)))
    lw = -jnp.exp(-sp - 0.5)
    ha = _mm(xa, a1_ref[...])
    za = a0_ref[...] + _mm(ha.astype(BF16), a2_ref[...])
    a = 1.0 / (1.0 + jnp.exp(-za))
    for q in range(lw_ref.shape[0]):
        lw_ref[q] = lw[:, q * SLAB:(q + 1) * SLAB]
        a_ref[q] = a[:, q * SLAB:(q + 1) * SLAB].astype(BF16)


def _rwkv_mix(x2d, mu, w0, w1, w2, a0, a1, a2, *, seq_len):
    bt, c = x2d.shape
    tm = min(256, seq_len)
    nq = c // SLAB
    kern = functools.partial(_rwkv_mix_kernel, seq_tiles=seq_len // tm)
    full = lambda shape: pl.BlockSpec(shape, lambda i: (0,) * len(shape))
    return pl.pallas_call(
        kern,
        grid=(bt // tm,),
        in_specs=[
            pl.BlockSpec((tm, c), lambda i: (i, 0)),
            pl.BlockSpec((8, c), lambda i: (jnp.maximum(i * (tm // 8) - 1, 0), 0)),
            full((6, c)),
            full((1, c)), full((c, LORA_PAD)), full((LORA_PAD, c)),
            full((1, c)), full((c, LORA_PAD)), full((LORA_PAD, c)),
        ],
        out_specs=[
            pl.BlockSpec((4, tm, c), lambda i: (0, i, 0)),
            pl.BlockSpec((nq, tm, SLAB), lambda i: (0, i, 0)),
            pl.BlockSpec((nq, tm, SLAB), lambda i: (0, i, 0)),
        ],
        out_shape=[
            jax.ShapeDtypeStruct((4, bt, c), BF16),
            jax.ShapeDtypeStruct((nq, bt, SLAB), F32),
            jax.ShapeDtypeStruct((nq, bt, SLAB), BF16),
        ],
        compiler_params=pltpu.CompilerParams(
            dimension_semantics=("arbitrary",),
            vmem_limit_bytes=VMEM_LIMIT),
    )(x2d, x2d, mu, w0, w1, w2, a0, a1, a2)


def _stream_matmul_kernel(x_ref, w_ref, o_ref):
    acc = _mm(x_ref[0], w_ref[0])
    for q in range(o_ref.shape[1]):
        o_ref[0, q] = acc[:, q * SLAB:(q + 1) * SLAB].astype(BF16)


def _stream_matmul(xs, w_in):
    ns, bt, c = xs.shape
    tm = min(2048, bt)
    tn = 1024
    return pl.pallas_call(
        _stream_matmul_kernel,
        grid=(bt // tm, ns, c // tn),
        in_specs=[
            pl.BlockSpec((1, tm, c), lambda i, s, n: (s, i, 0)),
            pl.BlockSpec((1, c, tn), lambda i, s, n: (s, 0, n)),
        ],
        out_specs=pl.BlockSpec((1, tn // SLAB, tm, SLAB), lambda i, s, n: (s, n, i, 0)),
        out_shape=jax.ShapeDtypeStruct((ns, c // SLAB, bt, SLAB), BF16),
        compiler_params=pltpu.CompilerParams(
            dimension_semantics=("arbitrary", "arbitrary", "arbitrary"),
            vmem_limit_bytes=VMEM_LIMIT),
    )(xs, w_in)


def _wkv_masks():
    ti = lax.broadcasted_iota(jnp.int32, (PAIR, PAIR), 0)
    si = lax.broadcasted_iota(jnp.int32, (PAIR, PAIR), 1)
    same = (ti // WKV_CHUNK) == (si // WKV_CHUNK)
    tl = ti % WKV_CHUNK
    sl = si % WKV_CHUNK
    levels = []
    b = 1
    while b < WKV_CHUNK:
        levels.append(same & ((tl // b) == (sl // b) + 1) & (((tl // b) % 2) == 1))
        b *= 2
    return dict(same=same, eye=ti == si,
                strict=same & (sl < tl), incl=same & (sl <= tl),
                anti_strict=(~same) & (sl < tl), anti_incl=(~same) & (sl <= tl),
                levels=levels)


def _wkv_kernel(r_ref, k_ref, v_ref, g_ref, lw_ref, a_ref, kk_ref, ka_ref, rk_ref, gng_ref, gnb_ref,
                o_ref, h_ref, rp_s, y0_s, m2_s, gt_s, plr_s, bonus_s):
    seq = o_ref.shape[1]
    ch = WKV_CHUNK
    gl = WKV_GROUP * ch
    masks = _wkv_masks()
    lane = lax.broadcasted_iota(jnp.int32, (ch, PAIR), 1)
    m0 = lane < RWKV_HEAD
    gt = lax.broadcasted_iota(jnp.int32, (gl, gl), 0)
    gs = lax.broadcasted_iota(jnp.int32, (gl, gl), 1)
    tri = jnp.where(((gt // ch) == (gs // ch)) & (gs <= gt), 1.0, 0.0).astype(BF16)

    glane = lax.broadcasted_iota(jnp.int32, (gl, PAIR), 1)
    g0 = glane < RWKV_HEAD

    def head_sum(x):
        s0 = jnp.sum(jnp.where(g0, x, 0.0), axis=1, keepdims=True)
        s1 = jnp.sum(jnp.where(g0, 0.0, x), axis=1, keepdims=True)
        return jnp.where(g0, s0, s1)

    @pl.when(pl.program_id(2) == 0)
    def _():
        h_ref[...] = jnp.zeros(h_ref.shape, F32)

    pps = SLAB // PAIR
    npair = o_ref.shape[0] * pps
    inv_n = 1.0 / RWKV_HEAD

    nunit = npair * WKV_GROUP

    def rows_of(gi):
        return pl.ds(pl.multiple_of(gi * gl, gl), gl)

    def state_pass(gi):
        slot = gi % 2
        rows = rows_of(gi)
        st = dict(hts=None, ys=[[] for _ in range(npair)])

        def chunk_step(j):
            if j == 0:
                st["hts"] = [h_ref[p] for p in range(npair)]
            for p in range(npair):
                ui = p * WKV_GROUP + j
                htb = st["hts"][p].astype(BF16)
                st["ys"][p].append(_mm_nt(rp_s[slot, ui], htb) + y0_s[slot, ui])
                st["hts"][p] = st["hts"][p] * plr_s[slot, ui] + _mm_nt(htb, m2_s[slot, ui]) + gt_s[slot, ui]
            if j == WKV_GROUP - 1:
                for p in range(npair):
                    h_ref[p] = st["hts"][p]

        def finish():
            for p in range(npair):
                sb = p // pps
                lanes = slice((p % pps) * PAIR, (p % pps + 1) * PAIR)
                y = jnp.concatenate(st["ys"][p], 0)
                mean = head_sum(y) * inv_n
                yc = y - mean
                var = head_sum(yc * yc) * inv_n
                yn = yc * lax.rsqrt(var + GN_EPS) * gng_ref[sb, :, lanes] + gnb_ref[sb, :, lanes]
                g = g_ref[0, sb, rows, lanes].astype(F32)
                out = (yn + bonus_s[slot, p]) * (g / (1.0 + jnp.exp(-g)))
                o_ref[sb, rows, lanes] = out.astype(BF16)

        return [functools.partial(chunk_step, j) for j in range(WKV_GROUP)] + [finish]

    def matmul_pass(gi, hooks):
        slot = gi % 2
        rows = rows_of(gi)
        hooks = list(hooks)

        def run_hook():
            if hooks:
                hooks.pop(0)()

        pre = []
        for p in range(npair):
            sb = p // pps
            lanes = slice((p % pps) * PAIR, (p % pps + 1) * PAIR)
            r = r_ref[0, sb, rows, lanes].astype(F32)
            k = k_ref[0, sb, rows, lanes].astype(F32)
            v = v_ref[0, sb, rows, lanes].astype(F32)
            lw = lw_ref[sb, rows, lanes]
            a = a_ref[sb, rows, lanes].astype(F32)
            l1, l2 = _split2(lw)
            c = _mm(tri, l1) + _mm(tri, l2)
            pw = jnp.exp(c)
            pinv = jnp.exp(-c)
            pex = jnp.exp(c - lw)
            kkr = k * kk_ref[sb, :, lanes]
            kk = kkr * lax.rsqrt(jnp.maximum(head_sum(kkr * kkr), 1e-24))
            kmod = k * (1.0 + (a - 1.0) * ka_ref[sb, :, lanes])
            bonus = head_sum(r * kmod * rk_ref[sb, :, lanes]) * v
            bonus_s[slot, p] = bonus
            pre.append(dict(v=v, pw=pw, rt=r * pw, bt=kk * pex, kt=kmod * pinv, at=a * kk * pinv))

        units = []
        for p in range(npair):
            d = pre[p]
            for j in range(WKV_GROUP):
                cs = slice(j * ch, (j + 1) * ch)
                units.append(dict(rt=d["rt"][cs], bt=d["bt"][cs], kt=d["kt"][cs], at=d["at"][cs],
                                  v=d["v"][cs], pl_row=d["pw"][j * ch + ch - 1:j * ch + ch, :]))

        z = jnp.zeros((ch, PAIR), F32)
        for u in units:
            rt_j, bt_j, kt_j, at_j = u["rt"], u["bt"], u["kt"], u["at"]
            lhs0 = jnp.concatenate([jnp.where(m0, rt_j, z), jnp.where(m0, bt_j, z)], 0).astype(BF16)
            lhs1 = jnp.concatenate([jnp.where(m0, z, rt_j), jnp.where(m0, z, bt_j)], 0).astype(BF16)
            rhs0 = jnp.concatenate([at_j, kt_j], 0).astype(BF16)
            rhs1 = jnp.concatenate([kt_j, at_j], 0).astype(BF16)
            u["a0"] = _mm_nt(lhs0, rhs0)
            u["a1"] = _mm_nt(lhs1, rhs1)
        for u in units:
            a0, a1 = u.pop("a0"), u.pop("a1")
            rblk = jnp.concatenate([a0[:ch], a1[:ch]], 0)
            bblk = jnp.concatenate([a0[ch:], a1[ch:]], 0)
            u["nmat"] = jnp.where(masks["strict"], bblk, 0.0)
            u["abk_anti"] = jnp.where(masks["anti_strict"], bblk, 0.0).astype(BF16)
            u["ara_bd"] = jnp.where(masks["incl"], rblk, 0.0).astype(BF16)
            u["ark_anti"] = jnp.where(masks["anti_incl"], rblk, 0.0).astype(BF16)
            v_j, bt_j = u["v"], u["bt"]
            u["v_sw"] = jnp.concatenate([jnp.where(m0, z, v_j), jnp.where(m0, v_j, z)], 0).astype(BF16)
            u["b_st"] = jnp.concatenate([jnp.where(m0, bt_j, z), jnp.where(m0, z, bt_j)], 0).astype(BF16)
        for u in units:
            u["w2"] = _mm(u["abk_anti"], u["v_sw"]).astype(BF16)
        run_hook()

        eye = jnp.where(masks["eye"], 1.0, 0.0).astype(F32)
        for u in units:
            u["t"] = eye - jnp.where(masks["levels"][0], u["nmat"], 0.0)
        for li, lvl in enumerate(masks["levels"][1:]):
            blk = 2 << li
            if blk % 8 != 0:
                for u in units:
                    u["tb"] = u["t"].astype(BF16)
                    u["x"] = _mm(u["tb"], jnp.where(lvl, u["nmat"], 0.0).astype(BF16)).astype(BF16)
                for u in units:
                    u["t"] = u["t"] - _mm(u.pop("x"), u.pop("tb"))
            else:
                nblk = PAIR // blk
                for u in units:
                    u["tb"] = u["t"].astype(BF16)
                    t_odd = jnp.concatenate([u["t"][i * blk:(i + 1) * blk] for i in range(1, nblk, 2)], 0)
                    u["x"] = _mm(t_odd.astype(BF16), jnp.where(lvl, u["nmat"], 0.0).astype(BF16)).astype(BF16)
                for u in units:
                    y = _mm(u.pop("x"), u.pop("tb"))
                    t = u["t"]
                    u["t"] = jnp.concatenate(
                        [t[i * blk:(i + 1) * blk] - y[(i // 2) * blk:(i // 2 + 1) * blk] if i % 2
                         else t[i * blk:(i + 1) * blk] for i in range(nblk)], 0)
            if li in (0, 2, 4):
                run_hook()

        for u in units:
            u["tbv"] = _mm(u["t"].astype(BF16), jnp.concatenate([u["b_st"], u["w2"]], 1))
        zero_st = jnp.zeros((PAIR, PAIR), BF16)
        for u in units:
            tbv = u["tbv"]
            bp_st = tbv[:, :PAIR]
            vp_st = tbv[:, PAIR:]
            rhs_y = jnp.concatenate([
                jnp.concatenate([u["v_sw"], zero_st], 1),
                jnp.concatenate([(-vp_st).astype(BF16), (-bp_st).astype(BF16)], 1)], 0)
            u["yr"] = _mm(jnp.concatenate([u["ark_anti"], u["ara_bd"]], 1), rhs_y)
            bp = bp_st[:ch] + bp_st[ch:]
            vp = vp_st[:ch] + vp_st[ch:]
            khat = u["kt"] * u["pl_row"]
            ahat = u["at"] * u["pl_row"]
            zc = jnp.zeros((ch, PAIR), BF16)
            lhs_t = jnp.concatenate([u["v"], -vp, -ahat], 0).astype(BF16)
            rhs_t = jnp.concatenate([
                jnp.concatenate([khat.astype(BF16), zc], 1),
                jnp.concatenate([ahat.astype(BF16), zc], 1),
                jnp.concatenate([zc, bp.astype(BF16)], 1)], 0)
            u["gm"] = _mm_tn(lhs_t, rhs_t)
        run_hook()
        for ui, u in enumerate(units):
            yr, gm = u["yr"], u["gm"]
            y0_s[slot, ui] = yr[:ch, :PAIR] + yr[ch:, :PAIR]
            rp_s[slot, ui] = (u["rt"] + yr[:ch, PAIR:] + yr[ch:, PAIR:]).astype(BF16)
            gt_s[slot, ui] = jnp.where(masks["same"], gm[:, :PAIR], 0.0)
            m2_s[slot, ui] = jnp.where(masks["same"], gm[:, PAIR:], 0.0).astype(BF16)
            plr_s[slot, ui] = u["pl_row"]
        while hooks:
            run_hook()

    ngroups = seq // gl
    matmul_pass(0, [])

    def group(gi, carry):
        matmul_pass(gi, state_pass(gi - 1))
        return carry

    lax.fori_loop(1, ngroups, group, 0)
    for hook in state_pass(ngroups - 1):
        hook()


def _wkv(proj, lw, a, kk, ka, rk, gng, gnb, *, batch, seq_len):
    nq = lw.shape[0]
    bt = lw.shape[1]
    ns = WKV_SLABS
    npair = ns * SLAB // PAIR
    nunit = npair * WKV_GROUP
    tb = min(WKV_TIME_BLOCK, seq_len)
    nt = seq_len // tb
    pspec = lambda s: pl.BlockSpec((1, ns, tb, SLAB), lambda b, q, t, s=s: (s, q, b * nt + t, 0))
    aspec = pl.BlockSpec((ns, tb, SLAB), lambda b, q, t: (q, b * nt + t, 0))
    vspec = pl.BlockSpec((ns, 1, SLAB), lambda b, q, t: (q, 0, 0))
    return pl.pallas_call(
        _wkv_kernel,
        grid=(batch, nq // ns, nt),
        in_specs=[pspec(0), pspec(1), pspec(2), pspec(3), aspec, aspec, vspec, vspec, vspec, vspec, vspec],
        out_specs=aspec,
        out_shape=jax.ShapeDtypeStruct((nq, bt, SLAB), BF16),
        scratch_shapes=[
            pltpu.VMEM((npair, PAIR, PAIR), F32),
            pltpu.VMEM((2, nunit, WKV_CHUNK, PAIR), BF16),
            pltpu.VMEM((2, nunit, WKV_CHUNK, PAIR), F32),
            pltpu.VMEM((2, nunit, PAIR, PAIR), BF16),
            pltpu.VMEM((2, nunit, PAIR, PAIR), F32),
            pltpu.VMEM((2, nunit, 1, PAIR), F32),
            pltpu.VMEM((2, npair, WKV_GROUP * WKV_CHUNK, PAIR), F32),
        ],
        compiler_params=pltpu.CompilerParams(
            dimension_semantics=("arbitrary", "arbitrary", "arbitrary"),
            vmem_limit_bytes=VMEM_LIMIT),
    )(proj, proj, proj, proj, lw, a, kk, ka, rk, gng, gnb)


def _out_ln_kernel(y_ref, x_ref, w_ref, g_ref, b_ref, *o_refs):
    acc = _mm(y_ref[0], w_ref[0])
    for q in range(1, y_ref.shape[0]):
        acc = acc + _mm(y_ref[q], w_ref[q])
    h = DEEPNORM_ALPHA * x_ref[...] + acc
    mu = jnp.mean(h, axis=-1, keepdims=True)
    hc = h - mu
    var = jnp.mean(hc * hc, axis=-1, keepdims=True)
    out = hc * lax.rsqrt(var + LN_EPS) * g_ref[...] + b_ref[...]
    o_refs[0][...] = out
    if len(o_refs) > 1:
        o_refs[1][...] = out.astype(BF16)


def _out_ln(y_slab, x2d, w_slab, ln_g, ln_b, *, with_bf16):
    nq, bt, _ = y_slab.shape
    c = x2d.shape[1]
    tm = min(256, bt)
    row = pl.BlockSpec((tm, c), lambda i: (i, 0))
    out_specs = [row]
    out_shape = [jax.ShapeDtypeStruct((bt, c), F32)]
    if with_bf16:
        out_specs.append(row)
        out_shape.append(jax.ShapeDtypeStruct((bt, c), BF16))
    return pl.pallas_call(
        _out_ln_kernel,
        grid=(bt // tm,),
        in_specs=[
            pl.BlockSpec((nq, tm, SLAB), lambda i: (0, i, 0)),
            row,
            pl.BlockSpec((nq, SLAB, c), lambda i: (0, 0, 0)),
            pl.BlockSpec((1, c), lambda i: (0, 0)),
            pl.BlockSpec((1, c), lambda i: (0, 0)),
        ],
        out_specs=out_specs,
        out_shape=out_shape,
        compiler_params=pltpu.CompilerParams(
            dimension_semantics=("arbitrary",),
            vmem_limit_bytes=VMEM_LIMIT),
    )(y_slab, x2d, w_slab, ln_g, ln_b)


def _slab_matmul_kernel(x_ref, w_ref, o_ref, *, q_tiles):
    acc = _mm(x_ref[...], w_ref[...])
    acc = acc * jnp.where(pl.program_id(1) < q_tiles, DIFF_HEAD_DIM ** -0.5 * LOG2E, 1.0)
    for q in range(o_ref.shape[0]):
        o_ref[q] = acc[:, q * SLAB:(q + 1) * SLAB].astype(BF16)


def _slab_matmul(x_bf, w_bf, *, q_cols):
    bt, c = x_bf.shape
    n_out = w_bf.shape[1]
    tm = min(2048, bt)
    tn = 1024
    return pl.pallas_call(
        functools.partial(_slab_matmul_kernel, q_tiles=q_cols // tn),
        grid=(bt // tm, n_out // tn),
        in_specs=[
            pl.BlockSpec((tm, c), lambda i, n: (i, 0)),
            pl.BlockSpec((c, tn), lambda i, n: (0, n)),
        ],
        out_specs=pl.BlockSpec((tn // SLAB, tm, SLAB), lambda i, n: (n, i, 0)),
        out_shape=jax.ShapeDtypeStruct((n_out // SLAB, bt, SLAB), BF16),
        compiler_params=pltpu.CompilerParams(
            dimension_semantics=("arbitrary", "arbitrary"),
            vmem_limit_bytes=VMEM_LIMIT),
    )(x_bf, w_bf)


def _attn_kernel(q_ref, k_ref, v_ref, g_ref, lam_ref, sg_ref, o_ref, m_ref, l_ref, acc_ref,
                 *, tq, lam_init):
    nh = q_ref.shape[0]
    hb = pl.program_id(1)
    qi = pl.program_id(2)
    dh = DIFF_HEAD_DIM
    slopes = [jnp.exp2(-(hb * nh + hh + 1).astype(F32)) * LOG2E for hh in range(nh)]
    lam_p = lam_ref[...]
    lam = (jnp.exp(jnp.sum(lam_p[0:1] * lam_p[1:2])) - jnp.exp(jnp.sum(lam_p[2:3] * lam_p[3:4]))
           + lam_init)

    ii = lax.broadcasted_iota(jnp.int32, (tq, tq), 0)
    jj = lax.broadcasted_iota(jnp.int32, (tq, tq), 1)
    dmat = (ii - jj).astype(F32)
    allowed = (jj // MASK_CHUNK) <= (ii // MASK_CHUNK)

    m_ref[...] = jnp.full(m_ref.shape, -jnp.inf, F32)
    l_ref[...] = jnp.zeros(l_ref.shape, F32)
    acc_ref[...] = jnp.zeros(acc_ref.shape, F32)

    nl = tq // 128
    streams = [(hh, u) for hh in range(nh) for u in range(2)]

    def step(blocks):
        nb = len(blocks)
        kbs = [[None] * nb for _ in range(nh)]
        vbs = [[None] * nb for _ in range(nh)]
        biases = [[None] * nb for _ in range(nh)]
        for b, (j, masked) in enumerate(blocks):
            k0 = pl.multiple_of(j * tq, tq)
            base = (k0 + ii).astype(F32) - jnp.abs(dmat) if masked else (k0 + jj[0:1, :]).astype(F32)
            for hh in range(nh):
                biases[hh][b] = slopes[hh] * base
                kbs[hh][b] = k_ref[hh, pl.ds(k0, tq), :]
                vbs[hh][b] = v_ref[hh, pl.ds(k0, tq), :]
        s = [[None] * nb for _ in streams]
        for w, (hh, u) in enumerate(streams):
            lanes = slice(u * dh, (u + 1) * dh)
            for b, (j, masked) in enumerate(blocks):
                su = _mm_nt(q_ref[hh, :, lanes], kbs[hh][b][:, lanes]) + biases[hh][b]
                if masked:
                    su = jnp.where(allowed, su, -jnp.inf)
                s[w][b] = su
        m_new, p = [], [[None] * nb for _ in streams]
        for w in range(len(streams)):
            pieces = [s[w][b][:, c * 128:(c + 1) * 128] for b in range(nb) for c in range(nl)]
            smax = functools.reduce(jnp.maximum, pieces)
            m_new.append(jnp.maximum(m_ref[w], jnp.max(smax, axis=1, keepdims=True)))
            mrep = jnp.concatenate([m_new[w]] * nl, axis=1)
            for b in range(nb):
                p[w][b] = jnp.exp2(s[w][b] - mrep)
        pv = [sum(_mm(p[w][b].astype(BF16), vbs[hh][b]) for b in range(nb))
              for w, (hh, u) in enumerate(streams)]
        for w in range(len(streams)):
            pieces = [p[w][b][:, c * 128:(c + 1) * 128] for b in range(nb) for c in range(nl)]
            psum = jnp.sum(functools.reduce(jnp.add, pieces), axis=1, keepdims=True)
            corr = jnp.exp2(m_ref[w] - m_new[w])
            l_ref[w] = corr * l_ref[w] + psum
            acc_ref[w] = jnp.concatenate([corr] * (SLAB // 128), axis=1) * acc_ref[w] + pv[w]
            m_ref[w] = m_new[w]

    def body(jp, carry):
        step([(2 * jp, False), (2 * jp + 1, False)])
        return carry

    lax.fori_loop(0, qi // 2, body, 0)

    @pl.when(qi % 2 == 1)
    def _():
        step([(qi - 1, False), (qi, True)])

    @pl.when(qi % 2 == 0)
    def _():
        step([(qi, True)])

    for hh in range(nh):
        inv_l = [1.0 / l_ref[2 * hh + u][:, 0:1] for u in range(2)]
        o = acc_ref[2 * hh] * inv_l[0] - lam * (acc_ref[2 * hh + 1] * inv_l[1])
        o = (o * lax.rsqrt(jnp.mean(o * o, axis=-1, keepdims=True) + SUBLN_EPS) * sg_ref[...]
             * (1.0 - lam_init))
        g = g_ref[hh].astype(F32)
        o_ref[hh] = (o * (g / (1.0 + jnp.exp(-g)))).astype(BF16)


def _diff_attention(qgkv, lam_p, subln_g, *, batch, seq_len, layer):
    nq = qgkv.shape[0] // 4
    bt = qgkv.shape[1]
    tq = min(256, seq_len)
    nh = ATTN_HEADS
    nhb = nq // nh
    nqb = seq_len // tq
    lam_init = 0.8 - 0.6 * math.exp(-0.3 * layer)
    kern = functools.partial(_attn_kernel, tq=tq, lam_init=lam_init)
    return pl.pallas_call(
        kern,
        grid=(batch, nhb, nqb),
        in_specs=[
            pl.BlockSpec((nh, tq, SLAB), lambda b, h, i: (h, b * nqb + i, 0)),
            pl.BlockSpec((nh, seq_len, SLAB), lambda b, h, i: (2 * nhb + h, b, 0)),
            pl.BlockSpec((nh, seq_len, SLAB), lambda b, h, i: (3 * nhb + h, b, 0)),
            pl.BlockSpec((nh, tq, SLAB), lambda b, h, i: (nhb + h, b * nqb + i, 0)),
            pl.BlockSpec(lam_p.shape, lambda b, h, i: (0, 0)),
            pl.BlockSpec((1, SLAB), lambda b, h, i: (0, 0)),
        ],
        out_specs=pl.BlockSpec((nh, tq, SLAB), lambda b, h, i: (h, b * nqb + i, 0)),
        out_shape=jax.ShapeDtypeStruct((nq, bt, SLAB), BF16),
        scratch_shapes=[pltpu.VMEM((2 * nh, tq, 128), F32), pltpu.VMEM((2 * nh, tq, 128), F32),
                        pltpu.VMEM((2 * nh, tq, SLAB), F32)],
        compiler_params=pltpu.CompilerParams(
            dimension_semantics=("arbitrary", "arbitrary", "arbitrary"),
            vmem_limit_bytes=VMEM_LIMIT),
    )(qgkv, qgkv, qgkv, qgkv, lam_p, subln_g)


def _pad_cols(w):
    return jnp.pad(w, ((0, 0), (0, LORA_PAD - w.shape[1])))


def _pad_rows(w):
    return jnp.pad(w, ((0, LORA_PAD - w.shape[0]), (0, 0)))


def kernel(x, a_mu_proj, a_mu_lora, a_w_in, a_w0, a_w1, a_w2, a_a0, a_a1, a_a2, a_k_k, a_k_a, a_r_k,
           a_gn_g, a_gn_b, a_w_out, w_k_shared, w_v_shared, b_w_qg, b_lambda, b_subln_g, b_w_out,
           ln_g, ln_b):
    batch, seq_len, c = x.shape
    assert a_w_in.shape[0] == 1 and b_w_qg.shape[0] == 1 and ln_g.shape[0] == DEPTH
    assert c % SLAB == 0 and seq_len % (WKV_CHUNK * WKV_GROUP) == 0
    bt = batch * seq_len
    nq = c // SLAB
    x2d = x.reshape(bt, c)
    slab_vec = lambda p: p.reshape(nq, 1, SLAB)
    slab_rows = lambda w: w.astype(BF16).reshape(nq, SLAB, w.shape[1])

    mu = jnp.concatenate([a_mu_proj[0], a_mu_lora[0]], axis=0)
    xs, lw, a = _rwkv_mix(
        x2d, mu,
        a_w0[0][None], _pad_cols(a_w1[0]).astype(BF16), _pad_rows(a_w2[0]).astype(BF16),
        a_a0[0][None], _pad_cols(a_a1[0]).astype(BF16), _pad_rows(a_a2[0]).astype(BF16),
        seq_len=seq_len)
    proj = _stream_matmul(xs, a_w_in[0].astype(BF16))
    yg = _wkv(proj, lw, a, slab_vec(a_k_k[0]), slab_vec(a_k_a[0]), slab_vec(a_r_k[0]),
              slab_vec(a_gn_g[0]), slab_vec(a_gn_b[0]), batch=batch, seq_len=seq_len)
    x1, x1_bf = _out_ln(yg, x2d, slab_rows(a_w_out[0]), ln_g[0][None], ln_b[0][None], with_bf16=True)

    w_all = jnp.concatenate([b_w_qg[0], w_k_shared, w_v_shared], axis=1).astype(BF16)
    qgkv = _slab_matmul(x1_bf, w_all, q_cols=c)
    og = _diff_attention(qgkv, b_lambda[0], b_subln_g[0][None], batch=batch, seq_len=seq_len, layer=1)
    (out,) = _out_ln(og, x1, slab_rows(b_w_out[0]), ln_g[1][None], ln_b[1][None], with_bf16=False)
    return out.reshape(batch, seq_len, c)
```

```python
import functools
import math

import jax
import jax.numpy as jnp
from jax import lax
from jax.experimental import pallas as pl
from jax.experimental.pallas import tpu as pltpu

F32 = jnp.float32
BF16 = jnp.bfloat16

DEPTH = 2
RWKV_HEAD = 64
MASK_CHUNK = 64
DIFF_HEAD_DIM = 128
LOG2E = math.log2(math.e)
GN_EPS = 64e-5
SUBLN_EPS = 1e-5
LN_EPS = 1e-5
DEEPNORM_ALPHA = (2.0 * DEPTH) ** 0.25

SLAB = 256
PAIR = 128
LORA_PAD = 128
WKV_CHUNK = 64
WKV_GROUP = 4
WKV_SLABS = 2
WKV_TIME_BLOCK = 2048
ATTN_HEADS = 4
VMEM_LIMIT = 56 * 1024 * 1024


def _mm(a, b):
    return jnp.dot(a, b, preferred_element_type=F32)


def _mm_nt(a, b):
    return lax.dot_general(a, b, (((1,), (1,)), ((), ())), preferred_element_type=F32)


def _mm_tn(a, b):
    return lax.dot_general(a, b, (((0,), (0,)), ((), ())), preferred_element_type=F32)


def _split2(x):
    hi = x.astype(BF16)
    lo = (x - hi.astype(F32)).astype(BF16)
    return hi, lo


def _rwkv_mix_kernel(x_ref, xp_ref, mu_ref, w0_ref, w1_ref, w2_ref, a0_ref, a1_ref, a2_ref,
                     xs_ref, lw_ref, a_ref, *, seq_tiles):
    i = pl.program_id(0)
    x = x_ref[...]
    prev_last = xp_ref[7:8, :]
    prev_last = jnp.where((i % seq_tiles) == 0, 0.0, prev_last)
    row = lax.broadcasted_iota(jnp.int32, x.shape, 0)
    xsh = jnp.where(row == 0, prev_last, pltpu.roll(x, 1, axis=0))
    xx = xsh - x
    for st in range(4):
        xs_ref[st] = (x + xx * mu_ref[st:st + 1, :]).astype(BF16)
    xw = (x + xx * mu_ref[4:5, :]).astype(BF16)
    xa = (x + xx * mu_ref[5:6, :]).astype(BF16)
    hw = jnp.tanh(_mm(xw, w1_ref[...]))
    z = w0_ref[...] + _mm(hw.astype(BF16), w2_ref[...])
    sp = jnp.maximum(-z, 0.0) + jnp.log(1.0 + jnp.exp(-jnp.abs(z)))
    lw = -jnp.exp(-sp - 0.5)
    ha = _mm(xa, a1_ref[...])
    za = a0_ref[...] + _mm(ha.astype(BF16), a2_ref[...])
    a = 1.0 / (1.0 + jnp.exp(-za))
    for q in range(lw_ref.shape[0]):
        lw_ref[q] = lw[:, q * SLAB:(q + 1) * SLAB]
        a_ref[q] = a[:, q * SLAB:(q + 1) * SLAB].astype(BF16)


def _rwkv_mix(x2d, mu, w0, w1, w2, a0, a1, a2, *, seq_len):
    bt, c = x2d.shape
    tm = min(256, seq_len)
    nq = c // SLAB
    kern = functools.partial(_rwkv_mix_kernel, seq_tiles=seq_len // tm)
    full = lambda shape: pl.BlockSpec(shape, lambda i: (0,) * len(shape))
    return pl.pallas_call(
        kern,
        grid=(bt // tm,),
        in_specs=[
            pl.BlockSpec((tm, c), lambda i: (i, 0)),
            pl.BlockSpec((8, c), lambda i: (jnp.maximum(i * (tm // 8) - 1, 0), 0)),
            full((6, c)),
            full((1, c)), full((c, LORA_PAD)), full((LORA_PAD, c)),
            full((1, c)), full((c, LORA_PAD)), full((LORA_PAD, c)),
        ],
        out_specs=[
            pl.BlockSpec((4, tm, c), lambda i: (0, i, 0)),
            pl.BlockSpec((nq, tm, SLAB), lambda i: (0, i, 0)),
            pl.BlockSpec((nq, tm, SLAB), lambda i: (0, i, 0)),
        ],
        out_shape=[
            jax.ShapeDtypeStruct((4, bt, c), BF16),
            jax.ShapeDtypeStruct((nq, bt, SLAB), F32),
            jax.ShapeDtypeStruct((nq, bt, SLAB), BF16),
        ],
        compiler_params=pltpu.CompilerParams(
            dimension_semantics=("arbitrary",),
            vmem_limit_bytes=VMEM_LIMIT),
    )(x2d, x2d, mu, w0, w1, w2, a0, a1, a2)


def _stream_matmul_kernel(x_ref, w_ref, o_ref):
    acc = _mm(x_ref[0], w_ref[0])
    for q in range(o_ref.shape[1]):
        o_ref[0, q] = acc[:, q * SLAB:(q + 1) * SLAB].astype(BF16)


def _stream_matmul(xs, w_in):
    ns, bt, c = xs.shape
    tm = min(2048, bt)
    tn = 1024
    return pl.pallas_call(
        _stream_matmul_kernel,
        grid=(bt // tm, ns, c // tn),
        in_specs=[
            pl.BlockSpec((1, tm, c), lambda i, s, n: (s, i, 0)),
            pl.BlockSpec((1, c, tn), lambda i, s, n: (s, 0, n)),
        ],
        out_specs=pl.BlockSpec((1, tn // SLAB, tm, SLAB), lambda i, s, n: (s, n, i, 0)),
        out_shape=jax.ShapeDtypeStruct((ns, c // SLAB, bt, SLAB), BF16),
        compiler_params=pltpu.CompilerParams(
            dimension_semantics=("arbitrary", "arbitrary", "arbitrary"),
            vmem_limit_bytes=VMEM_LIMIT),
    )(xs, w_in)


def _wkv_masks():
    ti = lax.broadcasted_iota(jnp.int32, (PAIR, PAIR), 0)
    si = lax.broadcasted_iota(jnp.int32, (PAIR, PAIR), 1)
    same = (ti // WKV_CHUNK) == (si // WKV_CHUNK)
    tl = ti % WKV_CHUNK
    sl = si % WKV_CHUNK
    levels = []
    b = 1
    while b < WKV_CHUNK:
        levels.append(same & ((tl // b) == (sl // b) + 1) & (((tl // b) % 2) == 1))
        b *= 2
    return dict(same=same, eye=ti == si,
                strict=same & (sl < tl), incl=same & (sl <= tl),
                anti_strict=(~same) & (sl < tl), anti_incl=(~same) & (sl <= tl),
                levels=levels)


def _wkv_kernel(r_ref, k_ref, v_ref, g_ref, lw_ref, a_ref, kk_ref, ka_ref, rk_ref, gng_ref, gnb_ref,
                o_ref, h_ref, rp_s, y0_s, m2_s, gt_s, plr_s, bonus_s):
    seq = o_ref.shape[1]
    ch = WKV_CHUNK
    gl = WKV_GROUP * ch
    masks = _wkv_masks()
    lane = lax.broadcasted_iota(jnp.int32, (ch, PAIR), 1)
    m0 = lane < RWKV_HEAD
    gt = lax.broadcasted_iota(jnp.int32, (gl, gl), 0)
    gs = lax.broadcasted_iota(jnp.int32, (gl, gl), 1)
    tri = jnp.where(((gt // ch) == (gs // ch)) & (gs <= gt), 1.0, 0.0).astype(BF16)

    glane = lax.broadcasted_iota(jnp.int32, (gl, PAIR), 1)
    g0 = glane < RWKV_HEAD

    def head_sum(x):
        s0 = jnp.sum(jnp.where(g0, x, 0.0), axis=1, keepdims=True)
        s1 = jnp.sum(jnp.where(g0, 0.0, x), axis=1, keepdims=True)
        return jnp.where(g0, s0, s1)

    @pl.when(pl.program_id(2) == 0)
    def _():
        h_ref[...] = jnp.zeros(h_ref.shape, F32)

    pps = SLAB // PAIR
    npair = o_ref.shape[0] * pps
    inv_n = 1.0 / RWKV_HEAD

    nunit = npair * WKV_GROUP

    def rows_of(gi):
        return pl.ds(pl.multiple_of(gi * gl, gl), gl)

    def state_pass(gi):
        slot = gi % 2
        rows = rows_of(gi)
        st = dict(hts=None, ys=[[] for _ in range(npair)])

        def chunk_step(j):
            if j == 0:
                st["hts"] = [h_ref[p] for p in range(npair)]
            for p in range(npair):
                ui = p * WKV_GROUP + j
                htb = st["hts"][p].astype(BF16)
                st["ys"][p].append(_mm_nt(rp_s[slot, ui], htb) + y0_s[slot, ui])
                st["hts"][p] = st["hts"][p] * plr_s[slot, ui] + _mm_nt(htb, m2_s[slot, ui]) + gt_s[slot, ui]
            if j == WKV_GROUP - 1:
                for p in range(npair):
                    h_ref[p] = st["hts"][p]

        def finish():
            for p in range(npair):
                sb = p // pps
                lanes = slice((p % pps) * PAIR, (p % pps + 1) * PAIR)
                y = jnp.concatenate(st["ys"][p], 0)
                mean = head_sum(y) * inv_n
                yc = y - mean
                var = head_sum(yc * yc) * inv_n
                yn = yc * lax.rsqrt(var + GN_EPS) * gng_ref[sb, :, lanes] + gnb_ref[sb, :, lanes]
                g = g_ref[0, sb, rows, lanes].astype(F32)
                out = (yn + bonus_s[slot, p]) * (g / (1.0 + jnp.exp(-g)))
                o_ref[sb, rows, lanes] = out.astype(BF16)

        return [functools.partial(chunk_step, j) for j in range(WKV_GROUP)] + [finish]

    def matmul_pass(gi, hooks):
        slot = gi % 2
        rows = rows_of(gi)
        hooks = list(hooks)

        def run_hook():
            if hooks:
                hooks.pop(0)()

        pre = []
        for p in range(npair):
            sb = p // pps
            lanes = slice((p % pps) * PAIR, (p % pps + 1) * PAIR)
            r = r_ref[0, sb, rows, lanes].astype(F32)
            k = k_ref[0, sb, rows, lanes].astype(F32)
            v = v_ref[0, sb, rows, lanes].astype(F32)
            lw = lw_ref[sb, rows, lanes]
            a = a_ref[sb, rows, lanes].astype(F32)
            l1, l2 = _split2(lw)
            c = _mm(tri, l1) + _mm(tri, l2)
            pw = jnp.exp(c)
            pinv = jnp.exp(-c)
            pex = jnp.exp(c - lw)
            kkr = k * kk_ref[sb, :, lanes]
            kk = kkr * lax.rsqrt(jnp.maximum(head_sum(kkr * kkr), 1e-24))
            kmod = k * (1.0 + (a - 1.0) * ka_ref[sb, :, lanes])
            bonus = head_sum(r * kmod * rk_ref[sb, :, lanes]) * v
            bonus_s[slot, p] = bonus
            pre.append(dict(v=v, pw=pw, rt=r * pw, bt=kk * pex, kt=kmod * pinv, at=a * kk * pinv))

        units = []
        for p in range(npair):
            d = pre[p]
            for j in range(WKV_GROUP):
                cs = slice(j * ch, (j + 1) * ch)
                units.append(dict(rt=d["rt"][cs], bt=d["bt"][cs], kt=d["kt"][cs], at=d["at"][cs],
                                  v=d["v"][cs], pl_row=d["pw"][j * ch + ch - 1:j * ch + ch, :]))

        z = jnp.zeros((ch, PAIR), F32)
        for u in units:
            rt_j, bt_j, kt_j, at_j = u["rt"], u["bt"], u["kt"], u["at"]
            lhs0 = jnp.concatenate([jnp.where(m0, rt_j, z), jnp.where(m0, bt_j, z)], 0).astype(BF16)
            lhs1 = jnp.concatenate([jnp.where(m0, z, rt_j), jnp.where(m0, z, bt_j)], 0).astype(BF16)
            rhs0 = jnp.concatenate([at_j, kt_j], 0).astype(BF16)
            rhs1 = jnp.concatenate([kt_j, at_j], 0).astype(BF16)
            u["a0"] = _mm_nt(lhs0, rhs0)
            u["a1"] = _mm_nt(lhs1, rhs1)
        for u in units:
            a0, a1 = u.pop("a0"), u.pop("a1")
            rblk = jnp.concatenate([a0[:ch], a1[:ch]], 0)
            bblk = jnp.concatenate([a0[ch:], a1[ch:]], 0)
            u["nmat"] = jnp.where(masks["strict"], bblk, 0.0)
            u["abk_anti"] = jnp.where(masks["anti_strict"], bblk, 0.0).astype(BF16)
            u["ara_bd"] = jnp.where(masks["incl"], rblk, 0.0).astype(BF16)
            u["ark_anti"] = jnp.where(masks["anti_incl"], rblk, 0.0).astype(BF16)
            v_j, bt_j = u["v"], u["bt"]
            u["v_sw"] = jnp.concatenate([jnp.where(m0, z, v_j), jnp.where(m0, v_j, z)], 0).astype(BF16)
            u["b_st"] = jnp.concatenate([jnp.where(m0, bt_j, z), jnp.where(m0, z, bt_j)], 0).astype(BF16)
        for u in units:
            u["w2"] = _mm(u["abk_anti"], u["v_sw"]).astype(BF16)
        run_hook()

        eye = jnp.where(masks["eye"], 1.0, 0.0).astype(F32)
        for u in units:
            u["t"] = eye - jnp.where(masks["levels"][0], u["nmat"], 0.0)
        for li, lvl in enumerate(masks["levels"][1:]):
            blk = 2 << li
            if blk % 8 != 0:
                for u in units:
                    u["tb"] = u["t"].astype(BF16)
                    u["x"] = _mm(u["tb"], jnp.where(lvl, u["nmat"], 0.0).astype(BF16)).astype(BF16)
                for u in units:
                    u["t"] = u["t"] - _mm(u.pop("x"), u.pop("tb"))
            else:
                nblk = PAIR // blk
                for u in units:
                    u["tb"] = u["t"].astype(BF16)
                    t_odd = jnp.concatenate([u["t"][i * blk:(i + 1) * blk] for i in range(1, nblk, 2)], 0)
                    u["x"] = _mm(t_odd.astype(BF16), jnp.where(lvl, u["nmat"], 0.0).astype(BF16)).astype(BF16)
                for u in units:
                    y = _mm(u.pop("x"), u.pop("tb"))
                    t = u["t"]
                    u["t"] = jnp.concatenate(
                        [t[i * blk:(i + 1) * blk] - y[(i // 2) * blk:(i // 2 + 1) * blk] if i % 2
                         else t[i * blk:(i + 1) * blk] for i in range(nblk)], 0)
            if li in (0, 2, 4):
                run_hook()

        for u in units:
            u["tbv"] = _mm(u["t"].astype(BF16), jnp.concatenate([u["b_st"], u["w2"]], 1))
        zero_st = jnp.zeros((PAIR, PAIR), BF16)
        for u in units:
            tbv = u["tbv"]
            bp_st = tbv[:, :PAIR]
            vp_st = tbv[:, PAIR:]
            rhs_y = jnp.concatenate([
                jnp.concatenate([u["v_sw"], zero_st], 1),
                jnp.concatenate([(-vp_st).astype(BF16), (-bp_st).astype(BF16)], 1)], 0)
            u["yr"] = _mm(jnp.concatenate([u["ark_anti"], u["ara_bd"]], 1), rhs_y)
            bp = bp_st[:ch] + bp_st[ch:]
            vp = vp_st[:ch] + vp_st[ch:]
            khat = u["kt"] * u["pl_row"]
            ahat = u["at"] * u["pl_row"]
            zc = jnp.zeros((ch, PAIR), BF16)
            lhs_t = jnp.concatenate([u["v"], -vp, -ahat], 0).astype(BF16)
            rhs_t = jnp.concatenate([
                jnp.concatenate([khat.astype(BF16), zc], 1),
                jnp.concatenate([ahat.astype(BF16), zc], 1),
                jnp.concatenate([zc, bp.astype(BF16)], 1)], 0)
            u["gm"] = _mm_tn(lhs_t, rhs_t)
        run_hook()
        for ui, u in enumerate(units):
            yr, gm = u["yr"], u["gm"]
            y0_s[slot, ui] = yr[:ch, :PAIR] + yr[ch:, :PAIR]
            rp_s[slot, ui] = (u["rt"] + yr[:ch, PAIR:] + yr[ch:, PAIR:]).astype(BF16)
            gt_s[slot, ui] = jnp.where(masks["same"], gm[:, :PAIR], 0.0)
            m2_s[slot, ui] = jnp.where(masks["same"], gm[:, PAIR:], 0.0).astype(BF16)
            plr_s[slot, ui] = u["pl_row"]
        while hooks:
            run_hook()

    ngroups = seq // gl
    matmul_pass(0, [])

    def group(gi, carry):
        matmul_pass(gi, state_pass(gi - 1))
        return carry

    lax.fori_loop(1, ngroups, group, 0)
    for hook in state_pass(ngroups - 1):
        hook()


def _wkv(proj, lw, a, kk, ka, rk, gng, gnb, *, batch, seq_len):
    nq = lw.shape[0]
    bt = lw.shape[1]
    ns = WKV_SLABS
    npair = ns * SLAB // PAIR
    nunit = npair * WKV_GROUP
    tb = min(WKV_TIME_BLOCK, seq_len)
    nt = seq_len // tb
    pspec = lambda s: pl.BlockSpec((1, ns, tb, SLAB), lambda b, q, t, s=s: (s, q, b * nt + t, 0))
    aspec = pl.BlockSpec((ns, tb, SLAB), lambda b, q, t: (q, b * nt + t, 0))
    vspec = pl.BlockSpec((ns, 1, SLAB), lambda b, q, t: (q, 0, 0))
    return pl.pallas_call(
        _wkv_kernel,
        grid=(batch, nq // ns, nt),
        in_specs=[pspec(0), pspec(1), pspec(2), pspec(3), aspec, aspec, vspec, vspec, vspec, vspec, vspec],
        out_specs=aspec,
        out_shape=jax.ShapeDtypeStruct((nq, bt, SLAB), BF16),
        scratch_shapes=[
            pltpu.VMEM((npair, PAIR, PAIR), F32),
            pltpu.VMEM((2, nunit, WKV_CHUNK, PAIR), BF16),
            pltpu.VMEM((2, nunit, WKV_CHUNK, PAIR), F32),
            pltpu.VMEM((2, nunit, PAIR, PAIR), BF16),
            pltpu.VMEM((2, nunit, PAIR, PAIR), F32),
            pltpu.VMEM((2, nunit, 1, PAIR), F32),
            pltpu.VMEM((2, npair, WKV_GROUP * WKV_CHUNK, PAIR), F32),
        ],
        compiler_params=pltpu.CompilerParams(
            dimension_semantics=("arbitrary", "arbitrary", "arbitrary"),
            vmem_limit_bytes=VMEM_LIMIT),
    )(proj, proj, proj, proj, lw, a, kk, ka, rk, gng, gnb)


def _out_ln_kernel(y_ref, x_ref, w_ref, g_ref, b_ref, *o_refs):
    acc = _mm(y_ref[0], w_ref[0])
    for q in range(1, y_ref.shape[0]):
        acc = acc + _mm(y_ref[q], w_ref[q])
    h = DEEPNORM_ALPHA * x_ref[...].astype(F32) + acc
    mu = jnp.mean(h, axis=-1, keepdims=True)
    hc = h - mu
    var = jnp.mean(hc * hc, axis=-1, keepdims=True)
    out = hc * lax.rsqrt(var + LN_EPS) * g_ref[...] + b_ref[...]
    o_refs[0][...] = out.astype(o_refs[0].dtype)


def _out_ln(y_slab, x2d, w_slab, ln_g, ln_b, *, out_dtype):
    nq, bt, _ = y_slab.shape
    c = x2d.shape[1]
    tm = min(256, bt)
    row = pl.BlockSpec((tm, c), lambda i: (i, 0))
    out_specs = [row]
    out_shape = [jax.ShapeDtypeStruct((bt, c), out_dtype)]
    return pl.pallas_call(
        _out_ln_kernel,
        grid=(bt // tm,),
        in_specs=[
            pl.BlockSpec((nq, tm, SLAB), lambda i: (0, i, 0)),
            row,
            pl.BlockSpec((nq, SLAB, c), lambda i: (0, 0, 0)),
            pl.BlockSpec((1, c), lambda i: (0, 0)),
            pl.BlockSpec((1, c), lambda i: (0, 0)),
        ],
        out_specs=out_specs,
        out_shape=out_shape,
        compiler_params=pltpu.CompilerParams(
            dimension_semantics=("arbitrary",),
            vmem_limit_bytes=VMEM_LIMIT),
    )(y_slab, x2d, w_slab, ln_g, ln_b)


def _slab_matmul_kernel(x_ref, w_ref, o_ref, *, q_tiles):
    acc = _mm(x_ref[...], w_ref[...])
    acc = acc * jnp.where(pl.program_id(1) < q_tiles, DIFF_HEAD_DIM ** -0.5 * LOG2E, 1.0)
    for q in range(o_ref.shape[0]):
        o_ref[q] = acc[:, q * SLAB:(q + 1) * SLAB].astype(BF16)


def _slab_matmul(x_bf, w_bf, *, q_cols):
    bt, c = x_bf.shape
    n_out = w_bf.shape[1]
    tm = min(2048, bt)
    tn = 1024
    return pl.pallas_call(
        functools.partial(_slab_matmul_kernel, q_tiles=q_cols // tn),
        grid=(bt // tm, n_out // tn),
        in_specs=[
            pl.BlockSpec((tm, c), lambda i, n: (i, 0)),
            pl.BlockSpec((c, tn), lambda i, n: (0, n)),
        ],
        out_specs=pl.BlockSpec((tn // SLAB, tm, SLAB), lambda i, n: (n, i, 0)),
        out_shape=jax.ShapeDtypeStruct((n_out // SLAB, bt, SLAB), BF16),
        compiler_params=pltpu.CompilerParams(
            dimension_semantics=("arbitrary", "arbitrary"),
            vmem_limit_bytes=VMEM_LIMIT),
    )(x_bf, w_bf)


def _attn_kernel(q_ref, k_ref, v_ref, g_ref, lam_ref, sg_ref, o_ref, m_ref, l_ref, acc_ref,
                 *, tq, lam_init):
    nh = q_ref.shape[0]
    hb = pl.program_id(1)
    qi = pl.program_id(2)
    dh = DIFF_HEAD_DIM
    slopes = [jnp.exp2(-(hb * nh + hh + 1).astype(F32)) * LOG2E for hh in range(nh)]
    lam_p = lam_ref[...]
    lam = (jnp.exp(jnp.sum(lam_p[0:1] * lam_p[1:2])) - jnp.exp(jnp.sum(lam_p[2:3] * lam_p[3:4]))
           + lam_init)

    ii = lax.broadcasted_iota(jnp.int32, (tq, tq), 0)
    jj = lax.broadcasted_iota(jnp.int32, (tq, tq), 1)
    dmat = (ii - jj).astype(F32)
    allowed = (jj // MASK_CHUNK) <= (ii // MASK_CHUNK)

    m_ref[...] = jnp.full(m_ref.shape, -jnp.inf, F32)
    l_ref[...] = jnp.zeros(l_ref.shape, F32)
    acc_ref[...] = jnp.zeros(acc_ref.shape, F32)

    nl = tq // 128
    streams = [(hh, u) for hh in range(nh) for u in range(2)]

    def step(blocks):
        nb = len(blocks)
        kbs = [[None] * nb for _ in range(nh)]
        vbs = [[None] * nb for _ in range(nh)]
        biases = [[None] * nb for _ in range(nh)]
        for b, (j, masked) in enumerate(blocks):
            k0 = pl.multiple_of(j * tq, tq)
            base = (k0 + ii).astype(F32) - jnp.abs(dmat) if masked else (k0 + jj[0:1, :]).astype(F32)
            for hh in range(nh):
                biases[hh][b] = slopes[hh] * base
                kbs[hh][b] = k_ref[hh, pl.ds(k0, tq), :]
                vbs[hh][b] = v_ref[hh, pl.ds(k0, tq), :]
        s = [[None] * nb for _ in streams]
        for w, (hh, u) in enumerate(streams):
            lanes = slice(u * dh, (u + 1) * dh)
            for b, (j, masked) in enumerate(blocks):
                su = _mm_nt(q_ref[hh, :, lanes], kbs[hh][b][:, lanes]) + biases[hh][b]
                if masked:
                    su = jnp.where(allowed, su, -jnp.inf)
                s[w][b] = su
        m_new, p = [], [[None] * nb for _ in streams]
        for w in range(len(streams)):
            pieces = [s[w][b][:, c * 128:(c + 1) * 128] for b in range(nb) for c in range(nl)]
            smax = functools.reduce(jnp.maximum, pieces)
            m_new.append(jnp.maximum(m_ref[w], jnp.max(smax, axis=1, keepdims=True)))
            mrep = jnp.concatenate([m_new[w]] * nl, axis=1)
            for b in range(nb):
                p[w][b] = jnp.exp2(s[w][b] - mrep)
        pv = [sum(_mm(p[w][b].astype(BF16), vbs[hh][b]) for b in range(nb))
              for w, (hh, u) in enumerate(streams)]
        for w in range(len(streams)):
            pieces = [p[w][b][:, c * 128:(c + 1) * 128] for b in range(nb) for c in range(nl)]
            psum = jnp.sum(functools.reduce(jnp.add, pieces), axis=1, keepdims=True)
            corr = jnp.exp2(m_ref[w] - m_new[w])
            l_ref[w] = corr * l_ref[w] + psum
            acc_ref[w] = jnp.concatenate([corr] * (SLAB // 128), axis=1) * acc_ref[w] + pv[w]
            m_ref[w] = m_new[w]

    def body(jp, carry):
        step([(2 * jp, False), (2 * jp + 1, False)])
        return carry

    lax.fori_loop(0, qi // 2, body, 0)

    @pl.when(qi % 2 == 1)
    def _():
        step([(qi - 1, False), (qi, True)])

    @pl.when(qi % 2 == 0)
    def _():
        step([(qi, True)])

    for hh in range(nh):
        inv_l = [1.0 / l_ref[2 * hh + u][:, 0:1] for u in range(2)]
        o = acc_ref[2 * hh] * inv_l[0] - lam * (acc_ref[2 * hh + 1] * inv_l[1])
        o = (o * lax.rsqrt(jnp.mean(o * o, axis=-1, keepdims=True) + SUBLN_EPS) * sg_ref[...]
             * (1.0 - lam_init))
        g = g_ref[hh].astype(F32)
        o_ref[hh] = (o * (g / (1.0 + jnp.exp(-g)))).astype(BF16)


def _diff_attention(qgkv, lam_p, subln_g, *, batch, seq_len, layer):
    nq = qgkv.shape[0] // 4
    bt = qgkv.shape[1]
    tq = min(256, seq_len)
    nh = ATTN_HEADS
    nhb = nq // nh
    nqb = seq_len // tq
    lam_init = 0.8 - 0.6 * math.exp(-0.3 * layer)
    kern = functools.partial(_attn_kernel, tq=tq, lam_init=lam_init)
    return pl.pallas_call(
        kern,
        grid=(batch, nhb, nqb),
        in_specs=[
            pl.BlockSpec((nh, tq, SLAB), lambda b, h, i: (h, b * nqb + i, 0)),
            pl.BlockSpec((nh, seq_len, SLAB), lambda b, h, i: (2 * nhb + h, b, 0)),
            pl.BlockSpec((nh, seq_len, SLAB), lambda b, h, i: (3 * nhb + h, b, 0)),
            pl.BlockSpec((nh, tq, SLAB), lambda b, h, i: (nhb + h, b * nqb + i, 0)),
            pl.BlockSpec(lam_p.shape, lambda b, h, i: (0, 0)),
            pl.BlockSpec((1, SLAB), lambda b, h, i: (0, 0)),
        ],
        out_specs=pl.BlockSpec((nh, tq, SLAB), lambda b, h, i: (h, b * nqb + i, 0)),
        out_shape=jax.ShapeDtypeStruct((nq, bt, SLAB), BF16),
        scratch_shapes=[pltpu.VMEM((2 * nh, tq, 128), F32), pltpu.VMEM((2 * nh, tq, 128), F32),
                        pltpu.VMEM((2 * nh, tq, SLAB), F32)],
        compiler_params=pltpu.CompilerParams(
            dimension_semantics=("arbitrary", "arbitrary", "arbitrary"),
            vmem_limit_bytes=VMEM_LIMIT),
    )(qgkv, qgkv, qgkv, qgkv, lam_p, subln_g)


def _pad_cols(w):
    return jnp.pad(w, ((0, 0), (0, LORA_PAD - w.shape[1])))


def _pad_rows(w):
    return jnp.pad(w, ((0, LORA_PAD - w.shape[0]), (0, 0)))


def kernel(x, a_mu_proj, a_mu_lora, a_w_in, a_w0, a_w1, a_w2, a_a0, a_a1, a_a2, a_k_k, a_k_a, a_r_k,
           a_gn_g, a_gn_b, a_w_out, w_k_shared, w_v_shared, b_w_qg, b_lambda, b_subln_g, b_w_out,
           ln_g, ln_b):
    batch, seq_len, c = x.shape
    assert a_w_in.shape[0] == 1 and b_w_qg.shape[0] == 1 and ln_g.shape[0] == DEPTH
    assert c % SLAB == 0 and seq_len % (WKV_CHUNK * WKV_GROUP) == 0
    bt = batch * seq_len
    nq = c // SLAB
    x2d = x.reshape(bt, c)
    slab_vec = lambda p: p.reshape(nq, 1, SLAB)
    slab_rows = lambda w: w.astype(BF16).reshape(nq, SLAB, w.shape[1])

    mu = jnp.concatenate([a_mu_proj[0], a_mu_lora[0]], axis=0)
    xs, lw, a = _rwkv_mix(
        x2d, mu,
        a_w0[0][None], _pad_cols(a_w1[0]).astype(BF16), _pad_rows(a_w2[0]).astype(BF16),
        a_a0[0][None], _pad_cols(a_a1[0]).astype(BF16), _pad_rows(a_a2[0]).astype(BF16),
        seq_len=seq_len)
    proj = _stream_matmul(xs, a_w_in[0].astype(BF16))
    yg = _wkv(proj, lw, a, slab_vec(a_k_k[0]), slab_vec(a_k_a[0]), slab_vec(a_r_k[0]),
              slab_vec(a_gn_g[0]), slab_vec(a_gn_b[0]), batch=batch, seq_len=seq_len)
    (x1_bf,) = _out_ln(yg, x2d, slab_rows(a_w_out[0]), ln_g[0][None], ln_b[0][None], out_dtype=BF16)

    w_all = jnp.concatenate([b_w_qg[0], w_k_shared, w_v_shared], axis=1).astype(BF16)
    qgkv = _slab_matmul(x1_bf, w_all, q_cols=c)
    og = _diff_attention(qgkv, b_lambda[0], b_subln_g[0][None], batch=batch, seq_len=seq_len, layer=1)
    (out,) = _out_ln(og, x1_bf, slab_rows(b_w_out[0]), ln_g[1][None], ln_b[1][None], out_dtype=F32)
    return out.reshape(batch, seq_len, c)
```
